```python
import jax, jax.numpy as jnp
from jax import lax
import numpy as np

D_MODEL = 1024
BATCH = 8
SEQ = 4096
DEPTH = 2
DEC_BATCH = 128
DEC_SEQ = 8
PAST_LEN = 16384
PAGE_SIZE = 128

N_MIXERS = 2
N_FOX = (DEPTH + 1) // 2
N_MLA = DEPTH // 2
FOX_HEADS = 16
FOX_KV_HEADS = 4
FOX_GROUP = FOX_HEADS // FOX_KV_HEADS
FOX_HEAD_DIM = 64
MLA_HEADS = 16
MLA_Q_LORA = 256
MLA_KV_LORA = 128
MLA_NOPE = 64
MLA_ROPE = 32
MLA_V_DIM = 64
ROPE_THETA = 10000.0
D_FF = 2816
CONV_W = 3

Q_BLOCK = 128
EPS = 1e-6

kernel_name = 'fox_mla_convffn_hybrid_step'


def rms_norm(x, w):
    xf = x.astype(jnp.float32)
    y = xf * lax.rsqrt(jnp.mean(xf * xf, axis=-1, keepdims=True) + EPS)
    return (y * w.astype(jnp.float32)).astype(x.dtype)


def rope(x, pos):
    half = MLA_ROPE // 2
    inv = ROPE_THETA ** (-jnp.arange(half, dtype=jnp.float32) / half)
    ang = pos.astype(jnp.float32)[:, None] * inv[None, :]
    cos = jnp.cos(ang)[None, :, None, :]
    sin = jnp.sin(ang)[None, :, None, :]
    xf = x.astype(jnp.float32)
    x1, x2 = xf[..., :half], xf[..., half:]
    return jnp.concatenate([x1 * cos - x2 * sin, x1 * sin + x2 * cos], axis=-1).astype(x.dtype)


def to_blocks(a):
    b, s = a.shape[:2]
    return jnp.moveaxis(a.reshape(b, s // Q_BLOCK, Q_BLOCK, *a.shape[2:]), 1, 0)


def from_blocks(o):
    nb, b, qb = o.shape[:3]
    return jnp.moveaxis(o, 0, 1).reshape(b, nb * qb, *o.shape[3:])


def fox_project(h, w_in, b_f):
    b, t, _ = h.shape
    proj = jnp.einsum('btd,de->bte', h, w_in)
    nq = FOX_HEADS * FOX_HEAD_DIM
    nkv = FOX_KV_HEADS * FOX_HEAD_DIM
    q = proj[..., :nq].reshape(b, t, FOX_KV_HEADS, FOX_GROUP, FOX_HEAD_DIM)
    k = proj[..., nq:nq + nkv].reshape(b, t, FOX_KV_HEADS, FOX_HEAD_DIM)
    v = proj[..., nq + nkv:nq + 2 * nkv].reshape(b, t, FOX_KV_HEADS, FOX_HEAD_DIM)
    logf = jax.nn.log_sigmoid((proj[..., nq + 2 * nkv:] + b_f).astype(jnp.float32))
    return q, k, v, logf


def fox_logits(q, k, c_q, c_k):
    s = jnp.einsum('bqkgd,bskd->bkgqs', q, k).astype(jnp.float32) * (FOX_HEAD_DIM ** -0.5)
    return s + jnp.transpose(c_q, (0, 2, 3, 1))[..., :, None] - jnp.transpose(c_k, (0, 2, 3, 1))[..., None, :]


def fox_prompt(h, w_in, b_f, w_o):
    b, s_len, _ = h.shape
    q, k, v, logf = fox_project(h, w_in, b_f)
    c = jnp.cumsum(logf, axis=1).reshape(b, s_len, FOX_KV_HEADS, FOX_GROUP)
    pos = jnp.arange(s_len)

    def block(args):
        q_b, c_b, pos_b = args
        s = fox_logits(q_b, k, c_b, c)
        s = jnp.where(pos[None, :] <= pos_b[:, None], s, -jnp.inf)
        p = jax.nn.softmax(s, axis=-1).astype(v.dtype)
        return jnp.einsum('bkgqs,bskd->bqkgd', p, v)

    o = lax.map(block, (to_blocks(q), to_blocks(c), pos.reshape(-1, Q_BLOCK)))
    o = from_blocks(o).reshape(b, s_len, FOX_HEADS * FOX_HEAD_DIM)
    return jnp.einsum('bte,ed->btd', o, w_o), k, v, logf


def fox_sample(h, cache_k, cache_v, cache_logf, page_table, li, w_in, b_f, w_o):
    bd, t, _ = h.shape
    q, k_new, v_new, logf_new = fox_project(h, w_in, b_f)
    k_past = cache_k[li, page_table].reshape(bd, -1, FOX_KV_HEADS, FOX_HEAD_DIM)
    v_past = cache_v[li, page_table].reshape(bd, -1, FOX_KV_HEADS, FOX_HEAD_DIM)
    logf_past = cache_logf[li, page_table].reshape(bd, -1, FOX_HEADS).astype(jnp.float32)
    n_past = k_past.shape[1]
    c_past = jnp.cumsum(logf_past, axis=1)
    c_new = c_past[:, -1:] + jnp.cumsum(logf_new, axis=1)
    gs = (bd, -1, FOX_KV_HEADS, FOX_GROUP)
    s_past = fox_logits(q, k_past, c_new.reshape(gs), c_past.reshape(gs))
    s_new = fox_logits(q, k_new, c_new.reshape(gs), c_new.reshape(gs))
    ti = jnp.arange(t)
    s_new = jnp.where(ti[None, :] <= ti[:, None], s_new, -jnp.inf)
    p = jax.nn.softmax(jnp.concatenate([s_past, s_new], axis=-1), axis=-1).astype(v_new.dtype)
    o = (jnp.einsum('bkgqs,bskd->bqkgd', p[..., :n_past], v_past)
         + jnp.einsum('bkgqs,bskd->bqkgd', p[..., n_past:], v_new))
    o = o.reshape(bd, t, FOX_HEADS * FOX_HEAD_DIM)
    return jnp.einsum('bte,ed->btd', o, w_o), k_new, v_new, logf_new


def mla_project(h, pos, w_a, q_norm, kv_norm, w_uq, w_uk):
    a = jnp.einsum('btd,de->bte', h, w_a)
    c_q = rms_norm(a[..., :MLA_Q_LORA], q_norm)
    c_kv = rms_norm(a[..., MLA_Q_LORA:MLA_Q_LORA + MLA_KV_LORA], kv_norm)
    k_pe = rope(a[..., MLA_Q_LORA + MLA_KV_LORA:][:, :, None, :], pos)[:, :, 0]
    q = jnp.einsum('btc,che->bthe', c_q, w_uq)
    q_pe = rope(q[..., MLA_NOPE:], pos)
    q_lat = jnp.einsum('bthn,chn->bthc', q[..., :MLA_NOPE], w_uk)
    return q_lat, q_pe, c_kv, k_pe


def mla_logits(q_lat, q_pe, c_kv, k_pe):
    s = jnp.einsum('bqhc,bsc->bhqs', q_lat, c_kv) + jnp.einsum('bqhr,bsr->bhqs', q_pe, k_pe)
    return s.astype(jnp.float32) * ((MLA_NOPE + MLA_ROPE) ** -0.5)


def mla_out(o_lat, w_uv, w_o):
    b, t = o_lat.shape[:2]
    o = jnp.einsum('bthc,chv->bthv', o_lat, w_uv).reshape(b, t, MLA_HEADS * MLA_V_DIM)
    return jnp.einsum('bte,ed->btd', o, w_o)


def mla_prompt(h, w_a, q_norm, kv_norm, w_uq, w_uk, w_uv, w_o):
    s_len = h.shape[1]
    pos = jnp.arange(s_len)
    q_lat, q_pe, c_kv, k_pe = mla_project(h, pos, w_a, q_norm, kv_norm, w_uq, w_uk)

    def block(args):
        ql_b, qp_b, pos_b = args
        s = mla_logits(ql_b, qp_b, c_kv, k_pe)
        s = jnp.where(pos[None, :] <= pos_b[:, None], s, -jnp.inf)
        p = jax.nn.softmax(s, axis=-1).astype(c_kv.dtype)
        return jnp.einsum('bhqs,bsc->bqhc', p, c_kv)

    o_lat = from_blocks(lax.map(block, (to_blocks(q_lat), to_blocks(q_pe), pos.reshape(-1, Q_BLOCK))))
    return mla_out(o_lat, w_uv, w_o), c_kv, k_pe


def mla_sample(h, cache_ckv, cache_kpe, page_table, li, w_a, q_norm, kv_norm, w_uq, w_uk, w_uv, w_o):
    bd, t, _ = h.shape
    n_past = page_table.shape[1] * PAGE_SIZE
    pos = n_past + jnp.arange(t)
    q_lat, q_pe, c_new, kpe_new = mla_project(h, pos, w_a, q_norm, kv_norm, w_uq, w_uk)
    c_past = cache_ckv[li, page_table].reshape(bd, n_past, MLA_KV_LORA)
    kpe_past = cache_kpe[li, page_table].reshape(bd, n_past, MLA_ROPE)
    s_past = mla_logits(q_lat, q_pe, c_past, kpe_past)
    s_new = mla_logits(q_lat, q_pe, c_new, kpe_new)
    ti = jnp.arange(t)
    s_new = jnp.where(ti[None, :] <= ti[:, None], s_new, -jnp.inf)
    p = jax.nn.softmax(jnp.concatenate([s_past, s_new], axis=-1), axis=-1).astype(c_new.dtype)
    o_lat = (jnp.einsum('bhqs,bsc->bqhc', p[..., :n_past], c_past)
             + jnp.einsum('bhqs,bsc->bqhc', p[..., n_past:], c_new))
    return mla_out(o_lat, w_uv, w_o), c_new, kpe_new


def conv_ffn(h, prev, w_in, conv_w, conv_b, w_out):
    t = h.shape[1]
    gu = jnp.einsum('btd,de->bte', h, w_in)
    g, u = gu[..., :D_FF], gu[..., D_FF:]
    gp = jnp.concatenate([prev.astype(g.dtype), g], axis=1)
    acc = conv_b
    for j in range(CONV_W):
        acc = acc + conv_w[j] * gp[:, j:j + t]
    out = jnp.einsum('btf,fd->btd', jax.nn.gelu(acc, approximate=True) * u, w_out)
    return out, gp[:, -(CONV_W - 1):]


def setup_inputs(seed: int = 0) -> dict:
    key = jax.random.key(seed)
    ks = jax.random.split(key, 32)
    f32 = jnp.float32
    n_pages = PAST_LEN // PAGE_SIZE
    n_used = DEC_BATCH * n_pages
    n_pool = n_used + n_used // 4

    def nrm(k, shape, scale=1.0):
        return jax.random.normal(k, shape, f32) * scale

    def gain(k, shape):
        return 1.0 + 0.05 * jax.random.normal(k, shape, f32)

    fox_in = FOX_HEADS * FOX_HEAD_DIM + 2 * FOX_KV_HEADS * FOX_HEAD_DIM + FOX_HEADS
    mla_a = MLA_Q_LORA + MLA_KV_LORA + MLA_ROPE
    page_table = jax.random.permutation(ks[8], n_pool)[:n_used].reshape(DEC_BATCH, n_pages).astype(jnp.int32)
    return {
        'x_prompt': nrm(ks[0], (BATCH, SEQ, D_MODEL)),
        'x_sample': nrm(ks[1], (DEC_BATCH, DEC_SEQ, D_MODEL)),
        'cache_fox_k': nrm(ks[2], (N_FOX, n_pool, PAGE_SIZE, FOX_KV_HEADS, FOX_HEAD_DIM)),
        'cache_fox_v': nrm(ks[3], (N_FOX, n_pool, PAGE_SIZE, FOX_KV_HEADS, FOX_HEAD_DIM)),
        'cache_fox_logf': jax.nn.log_sigmoid(2.0 + nrm(ks[4], (N_FOX, n_pool, PAGE_SIZE, FOX_HEADS))),
        'cache_mla_ckv': nrm(ks[5], (N_MLA, n_pool, PAGE_SIZE, MLA_KV_LORA)),
        'cache_mla_kpe': nrm(ks[6], (N_MLA, n_pool, PAGE_SIZE, MLA_ROPE)),
        'state_conv': nrm(ks[7], (DEPTH, DEC_BATCH, CONV_W - 1, D_FF)),
        'page_table': page_table,
        'norm_mix_pre': gain(ks[9], (DEPTH, D_MODEL)),
        'norm_mix_post': gain(ks[10], (DEPTH, D_MODEL)),
        'norm_ffn_pre': gain(ks[11], (DEPTH, D_MODEL)),
        'norm_ffn_post': gain(ks[12], (DEPTH, D_MODEL)),
        'fox_w_in': nrm(ks[13], (N_FOX, D_MODEL, fox_in), D_MODEL ** -0.5),
        'fox_b_f': 2.0 + nrm(ks[14], (N_FOX, FOX_HEADS), 0.1),
        'fox_w_o': nrm(ks[15], (N_FOX, FOX_HEADS * FOX_HEAD_DIM, D_MODEL), (FOX_HEADS * FOX_HEAD_DIM) ** -0.5),
        'mla_w_a': nrm(ks[16], (N_MLA, D_MODEL, mla_a), D_MODEL ** -0.5),
        'mla_q_norm': gain(ks[17], (N_MLA, MLA_Q_LORA)),
        'mla_kv_norm': gain(ks[18], (N_MLA, MLA_KV_LORA)),
        'mla_w_uq': nrm(ks[19], (N_MLA, MLA_Q_LORA, MLA_HEADS, MLA_NOPE + MLA_ROPE), MLA_Q_LORA ** -0.5),
        'mla_w_uk': nrm(ks[20], (N_MLA, MLA_KV_LORA, MLA_HEADS, MLA_NOPE), MLA_KV_LORA ** -0.5),
        'mla_w_uv': nrm(ks[21], (N_MLA, MLA_KV_LORA, MLA_HEADS, MLA_V_DIM), MLA_KV_LORA ** -0.5),
        'mla_w_o': nrm(ks[22], (N_MLA, MLA_HEADS * MLA_V_DIM, D_MODEL), (MLA_HEADS * MLA_V_DIM) ** -0.5),
        'ffn_w_in': nrm(ks[23], (DEPTH, D_MODEL, 2 * D_FF), D_MODEL ** -0.5),
        'ffn_conv_w': nrm(ks[24], (DEPTH, CONV_W, D_FF), CONV_W ** -0.5),
        'ffn_conv_b': nrm(ks[25], (DEPTH, D_FF), 0.02),
        'ffn_w_out': nrm(ks[26], (DEPTH, D_FF, D_MODEL), D_FF ** -0.5),
    }


def reference(x_prompt, x_sample, cache_fox_k, cache_fox_v, cache_fox_logf, cache_mla_ckv, cache_mla_kpe,
              state_conv, page_table, norm_mix_pre, norm_mix_post, norm_ffn_pre, norm_ffn_post,
              fox_w_in, fox_b_f, fox_w_o, mla_w_a, mla_q_norm, mla_kv_norm, mla_w_uq, mla_w_uk, mla_w_uv,
              mla_w_o, ffn_w_in, ffn_conv_w, ffn_conv_b, ffn_w_out):
    xp, xs = x_prompt, x_sample
    fk_p, fv_p, ff_p, mc_p, mr_p, cv_p = [], [], [], [], [], []
    fk_s, fv_s, ff_s, mc_s, mr_s, cv_s = [], [], [], [], [], []
    for i in range(DEPTH):
        li = i // N_MIXERS
        hp = rms_norm(xp, norm_mix_pre[i])
        hs = rms_norm(xs, norm_mix_pre[i])
        if i % N_MIXERS == 0:
            yp, kp, vp, lp = fox_prompt(hp, fox_w_in[li], fox_b_f[li], fox_w_o[li])
            ys, ks_, vs_, ls_ = fox_sample(hs, cache_fox_k, cache_fox_v, cache_fox_logf, page_table, li,
                                           fox_w_in[li], fox_b_f[li], fox_w_o[li])
            fk_p.append(kp); fv_p.append(vp); ff_p.append(lp)
            fk_s.append(ks_); fv_s.append(vs_); ff_s.append(ls_)
        else:
            yp, cp, rp = mla_prompt(hp, mla_w_a[li], mla_q_norm[li], mla_kv_norm[li], mla_w_uq[li],
                                    mla_w_uk[li], mla_w_uv[li], mla_w_o[li])
            ys, cs_, rs_ = mla_sample(hs, cache_mla_ckv, cache_mla_kpe, page_table, li, mla_w_a[li],
                                      mla_q_norm[li], mla_kv_norm[li], mla_w_uq[li], mla_w_uk[li],
                                      mla_w_uv[li], mla_w_o[li])
            mc_p.append(cp); mr_p.append(rp)
            mc_s.append(cs_); mr_s.append(rs_)
        xp = xp + rms_norm(yp, norm_mix_post[i])
        xs = xs + rms_norm(ys, norm_mix_post[i])
        hp = rms_norm(xp, norm_ffn_pre[i])
        hs = rms_norm(xs, norm_ffn_pre[i])
        zero_prev = jnp.zeros((xp.shape[0], CONV_W - 1, D_FF), xp.dtype)
        yp, cvp = conv_ffn(hp, zero_prev, ffn_w_in[i], ffn_conv_w[i], ffn_conv_b[i], ffn_w_out[i])
        ys, cvs = conv_ffn(hs, state_conv[i], ffn_w_in[i], ffn_conv_w[i], ffn_conv_b[i], ffn_w_out[i])
        cv_p.append(cvp); cv_s.append(cvs)
        xp = xp + rms_norm(yp, norm_ffn_post[i])
        xs = xs + rms_norm(ys, norm_ffn_post[i])
    ldt = cache_fox_logf.dtype
    return (xp, xs,
            jnp.stack(fk_p), jnp.stack(fv_p), jnp.stack(ff_p).astype(ldt), jnp.stack(mc_p), jnp.stack(mr_p), jnp.stack(cv_p),
            jnp.stack(fk_s), jnp.stack(fv_s), jnp.stack(ff_s).astype(ldt), jnp.stack(mc_s), jnp.stack(mr_s), jnp.stack(cv_s))
```

```python
import functools

import jax
import jax.numpy as jnp
from jax import lax
from jax.experimental import pallas as pl
from jax.experimental.pallas import tpu as pltpu

F32 = jnp.float32
BF16 = jnp.bfloat16

N_MIXERS = 2
FOX_HEADS = 16
FOX_KV_HEADS = 4
FOX_GROUP = FOX_HEADS // FOX_KV_HEADS
FOX_HEAD_DIM = 64
FOX_NQ = FOX_HEADS * FOX_HEAD_DIM
FOX_NKV = FOX_KV_HEADS * FOX_HEAD_DIM
MLA_HEADS = 16
MLA_Q_LORA = 256
MLA_KV_LORA = 128
MLA_NOPE = 64
MLA_ROPE = 32
MLA_V_DIM = 64
ROPE_THETA = 10000.0
CONV_W = 3
PAGE_SIZE = 128
EPS = 1e-6

LANES = 128
SUBLANES = 8
V7X_VMEM_LIMIT_BYTES = 56 * 1024 * 1024

MASKED = -1e30
MLA_QK = 2 * LANES
ROPE_REP = LANES // MLA_ROPE

NT_DIMS = (((1,), (1,)), ((), ()))


def _params(sem):
    return pltpu.CompilerParams(dimension_semantics=sem, vmem_limit_bytes=V7X_VMEM_LIMIT_BYTES)


def _tile(n, pref):
    t = min(n, pref)
    assert n % t == 0, (n, pref)
    return t


def _rms(x, g):
    return x * lax.rsqrt(jnp.mean(x * x, axis=-1, keepdims=True) + EPS) * g


def _log_sigmoid(x):
    return jnp.minimum(x, 0.0) - jnp.log1p(jnp.exp(-jnp.abs(x)))


def _split3(x):
    hi = x.astype(BF16)
    r1 = x - hi.astype(F32)
    mid = r1.astype(BF16)
    lo = (r1 - mid.astype(F32)).astype(BF16)
    return hi, mid, lo


def _dot(a, b):
    return jnp.dot(a, b, preferred_element_type=F32)


def _dot_nt(a, b):
    return lax.dot_general(a, b, NT_DIMS, preferred_element_type=F32)


def _fox_proj_kernel(x_ref, g_ref, w_ref, wft_ref, bf_ref, bft_ref, u_ref, l_ref,
                     q_ref, k_ref, v_ref, kbf_ref, vbf_ref, logf_ref, c_ref, ct_ref,
                     carry_ref, carryt_ref):
    tm = x_ref.shape[0]

    @pl.when(pl.program_id(1) == 0)
    def _():
        carry_ref[...] = jnp.zeros_like(carry_ref)
        carryt_ref[...] = jnp.zeros_like(carryt_ref)

    h = _rms(x_ref[...], g_ref[...]).astype(BF16)
    proj = _dot(h, w_ref[...])
    q_ref[...] = (proj[:, :FOX_NQ] * (FOX_HEAD_DIM ** -0.5)).astype(q_ref.dtype)
    k = proj[:, FOX_NQ:FOX_NQ + FOX_NKV]
    v = proj[:, FOX_NQ + FOX_NKV:FOX_NQ + 2 * FOX_NKV]
    k_ref[...] = k
    v_ref[...] = v
    for kh in range(FOX_KV_HEADS):
        sl = slice(kh * FOX_HEAD_DIM, (kh + 1) * FOX_HEAD_DIM)
        kbf_ref[kh] = k[:, sl].astype(BF16)
        vbf_ref[kh] = v[:, sl].astype(BF16)
    f0 = FOX_NQ + 2 * FOX_NKV
    logf = _log_sigmoid(proj[:, f0:f0 + FOX_HEADS] + bf_ref[...])
    logf_ref[...] = logf
    lmat = l_ref[...]
    c = carry_ref[...]
    for piece in _split3(logf):
        c = c + _dot(lmat, piece)
    for kh in range(FOX_KV_HEADS):
        c_ref[kh] = c[:, kh * FOX_GROUP:(kh + 1) * FOX_GROUP]
    carry_ref[...] = c[tm - 1:tm, :]
    logft = _log_sigmoid(_dot_nt(wft_ref[...], h) + bft_ref[...])
    umat = u_ref[...]
    ct = carryt_ref[...]
    for piece in _split3(logft):
        ct = ct + _dot(piece, umat)
    ct_ref[...] = ct
    carryt_ref[...] = ct[:, tm - 1:tm]


def _fox_project(x, g, w_in, b_f, *, seq_len, q_dtype):
    n, d = x.shape
    tm = _tile(n, 512)
    if seq_len >= tm:
        assert seq_len % tm == 0
        grid = (n // seq_len, seq_len // tm)
        grp = jnp.zeros((tm,), jnp.int32)
    else:
        assert tm % seq_len == 0
        grid = (n // tm, 1)
        grp = jnp.arange(tm, dtype=jnp.int32) // seq_len
    g1 = grid[1]
    pos = jnp.arange(tm, dtype=jnp.int32)
    same = grp[:, None] == grp[None, :]
    umat = (same & (pos[:, None] <= pos[None, :])).astype(BF16)
    lmat = umat.T
    f0 = FOX_NQ + 2 * FOX_NKV
    w_main = jnp.pad(w_in, ((0, 0), (0, LANES - FOX_HEADS))).astype(BF16)
    wft = w_in[:, f0:].T.astype(BF16)
    wcols = w_main.shape[1]

    tok = lambda i, j: (i * g1 + j, 0)
    tok3 = lambda i, j: (0, i * g1 + j, 0)
    const = lambda i, j: (0, 0)
    outs = pl.pallas_call(
        _fox_proj_kernel,
        grid=grid,
        in_specs=[
            pl.BlockSpec((tm, d), tok),
            pl.BlockSpec((1, d), const),
            pl.BlockSpec((d, wcols), const),
            pl.BlockSpec((FOX_HEADS, d), const),
            pl.BlockSpec((1, FOX_HEADS), const),
            pl.BlockSpec((FOX_HEADS, 1), const),
            pl.BlockSpec((tm, tm), const),
            pl.BlockSpec((tm, tm), const),
        ],
        out_specs=[
            pl.BlockSpec((tm, FOX_NQ), tok),
            pl.BlockSpec((tm, FOX_NKV), tok),
            pl.BlockSpec((tm, FOX_NKV), tok),
            pl.BlockSpec((FOX_KV_HEADS, tm, FOX_HEAD_DIM), tok3),
            pl.BlockSpec((FOX_KV_HEADS, tm, FOX_HEAD_DIM), tok3),
            pl.BlockSpec((tm, FOX_HEADS), tok),
            pl.BlockSpec((FOX_KV_HEADS, tm, FOX_GROUP), tok3),
            pl.BlockSpec((FOX_HEADS, tm), lambda i, j: (0, i * g1 + j)),
        ],
        out_shape=[
            jax.ShapeDtypeStruct((n, FOX_NQ), q_dtype),
            jax.ShapeDtypeStruct((n, FOX_NKV), F32),
            jax.ShapeDtypeStruct((n, FOX_NKV), F32),
            jax.ShapeDtypeStruct((FOX_KV_HEADS, n, FOX_HEAD_DIM), BF16),
            jax.ShapeDtypeStruct((FOX_KV_HEADS, n, FOX_HEAD_DIM), BF16),
            jax.ShapeDtypeStruct((n, FOX_HEADS), F32),
            jax.ShapeDtypeStruct((FOX_KV_HEADS, n, FOX_GROUP), F32),
            jax.ShapeDtypeStruct((FOX_HEADS, n), F32),
        ],
        scratch_shapes=[pltpu.VMEM((1, FOX_HEADS), F32), pltpu.VMEM((FOX_HEADS, 1), F32)],
        compiler_params=_params(("arbitrary", "arbitrary")),
        name="fox_proj",
    )(x, g.reshape(1, d), w_main, wft, b_f.reshape(1, FOX_HEADS), b_f.reshape(FOX_HEADS, 1), umat, lmat)
    return outs


def _online_softmax_step(s, v, m, l, acc):
    rows, n = acc.shape[0], s.shape[-1]
    m_new = jnp.maximum(m, jnp.max(s, axis=-1, keepdims=True))
    alpha = jnp.exp(m - m_new)
    p = jnp.exp(s - m_new)
    l_new = alpha * l + jnp.sum(p, axis=-1, keepdims=True)
    pv = _dot(p.reshape(rows, n).astype(BF16), v)
    return m_new, l_new, alpha.reshape(rows, 1) * acc + pv


def _fox_attn_kernel(q_ref, k_ref, v_ref, c_ref, ct_ref, o_ref, *, tq):
    qi = pl.program_id(2)
    g, hd = FOX_GROUP, FOX_HEAD_DIM
    q = q_ref[0]
    qs = jnp.concatenate([q[:, i * hd:(i + 1) * hd] for i in range(g)], axis=0)
    cq = c_ref[0, 0]
    cq3 = jnp.stack([cq[:, i:i + 1] for i in range(g)], axis=0)

    def step(j, carry, masked):
        m, l, acc = carry
        off = pl.multiple_of(j * tq, tq)
        k = k_ref[0, 0, pl.ds(off, tq), :]
        v = v_ref[0, 0, pl.ds(off, tq), :]
        ck = ct_ref[0, :, pl.ds(off, tq)]
        s3 = _dot_nt(qs, k).reshape(g, tq, tq) + cq3 - ck[:, None, :]
        if masked:
            row = lax.broadcasted_iota(jnp.int32, (tq, tq), 0)
            col = lax.broadcasted_iota(jnp.int32, (tq, tq), 1)
            s3 = jnp.where((col <= row)[None], s3, MASKED)
        return _online_softmax_step(s3, v, m, l, acc)

    init = (jnp.full((g, tq, 1), MASKED, F32), jnp.zeros((g, tq, 1), F32), jnp.zeros((g * tq, hd), F32))
    carry = lax.fori_loop(0, qi, lambda j, c: step(j, c, False), init)
    _, l, acc = step(qi, carry, True)
    o = acc / l.reshape(g * tq, 1)
    o_ref[0] = jnp.concatenate([o[i * tq:(i + 1) * tq] for i in range(g)], axis=1).astype(o_ref.dtype)


def _fox_prompt_attention(q, kbf, vbf, c, ct, *, batch, seq_len):
    n = batch * seq_len
    tq = _tile(seq_len, 256)
    gw = FOX_GROUP * FOX_HEAD_DIM
    q3 = q.reshape(batch, seq_len, FOX_NQ)
    k4 = kbf.reshape(FOX_KV_HEADS, batch, seq_len, FOX_HEAD_DIM)
    v4 = vbf.reshape(FOX_KV_HEADS, batch, seq_len, FOX_HEAD_DIM)
    c4 = c.reshape(FOX_KV_HEADS, batch, seq_len, FOX_GROUP)
    ct3 = ct.reshape(FOX_KV_HEADS, FOX_GROUP, n)
    o = pl.pallas_call(
        functools.partial(_fox_attn_kernel, tq=tq),
        grid=(batch, FOX_KV_HEADS, seq_len // tq),
        in_specs=[
            pl.BlockSpec((1, tq, gw), lambda b, h, i: (b, i, h)),
            pl.BlockSpec((1, 1, seq_len, FOX_HEAD_DIM), lambda b, h, i: (h, b, 0, 0)),
            pl.BlockSpec((1, 1, seq_len, FOX_HEAD_DIM), lambda b, h, i: (h, b, 0, 0)),
            pl.BlockSpec((1, 1, tq, FOX_GROUP), lambda b, h, i: (h, b, i, 0)),
            pl.BlockSpec((1, FOX_GROUP, seq_len), lambda b, h, i: (h, 0, b)),
        ],
        out_specs=pl.BlockSpec((1, tq, gw), lambda b, h, i: (b, i, h)),
        out_shape=jax.ShapeDtypeStruct((batch, seq_len, FOX_NQ), BF16),
        compiler_params=_params(("parallel", "parallel", "arbitrary")),
        name="fox_prompt_attn",
    )(q3, k4, v4, c4, ct3)
    return o.reshape(n, FOX_NQ)


def _page_copies(pt_ref, b, chunk, slot, ppc, pairs):
    out = []
    for i in range(ppc):
        page = pt_ref[b, chunk * ppc + i]
        for hbm, buf, sem in pairs:
            out.append(pltpu.make_async_copy(hbm.at[page], buf.at[slot, i], sem.at[slot]))
    return out


def _pad_rows(x, rows):
    return jnp.concatenate([x, jnp.zeros((rows - x.shape[0], x.shape[1]), x.dtype)], axis=0)


def _suffix_sum_lanes(x):
    lane = lax.broadcasted_iota(jnp.int32, x.shape, 1)
    y = x
    sh = 1
    while sh < LANES:
        y = y + jnp.where(lane < LANES - sh, pltpu.roll(y, LANES - sh, axis=1), 0.0)
        sh *= 2
    return y


def _fox_sample_kernel(pt_ref, q_ref, kn_ref, vn_ref, dcol_ref, dtp_ref, kt_hbm, vt_hbm, lf_hbm,
                       o_ref, kbuf, vbuf, lbuf, ksem, vsem, lsem, *, ppc, nchunks, t_new):
    b = pl.program_id(0)
    nk = ppc * PAGE_SIZE
    hd, g, heads = FOX_HEAD_DIM, FOX_GROUP, FOX_HEADS
    rows = heads * t_new
    pairs = ((kt_hbm, kbuf, ksem), (vt_hbm, vbuf, vsem), (lf_hbm, lbuf, lsem))

    for cp in _page_copies(pt_ref, b, nchunks - 1, 0, ppc, pairs):
        cp.start()

    q = q_ref[...]
    lane = lax.broadcasted_iota(jnp.int32, (t_new, FOX_NKV), 1)
    qrows = []
    for h in range(heads):
        kh, gi = divmod(h, g)
        grp = q[:, kh * FOX_NKV:(kh + 1) * FOX_NKV]
        shift = ((kh - gi) * hd) % FOX_NKV
        r = pltpu.roll(grp, shift, axis=1) if shift else grp
        qrows.append(jnp.where((lane >= kh * hd) & (lane < (kh + 1) * hd), r, 0.0))
    qbd = jnp.concatenate(qrows, axis=0).astype(BF16)
    dcol3 = dcol_ref[0].reshape(heads, t_new, 1)

    def body(ci, carry):
        m, l, acc, run = carry
        chunk = nchunks - 1 - ci
        slot = ci % 2

        @pl.when(ci + 1 < nchunks)
        def _():
            for cp in _page_copies(pt_ref, b, chunk - 1, 1 - slot, ppc, pairs):
                cp.start()

        for cp in _page_copies(pt_ref, b, chunk, slot, ppc, pairs):
            cp.wait()
        s_pages = [None] * ppc
        for i in reversed(range(ppc)):
            lf = lbuf[slot, i]
            y = _suffix_sum_lanes(lf)
            bias = y - lf + run
            run = run + y[:, 0:1]
            s = _dot(qbd, kbuf[slot, i].astype(BF16))
            s_pages[i] = s.reshape(heads, t_new, PAGE_SIZE) + dcol3 + bias[:, None, :]
        s3 = jnp.concatenate(s_pages, axis=2)
        m_new = jnp.maximum(m, jnp.max(s3, axis=-1, keepdims=True))
        alpha = jnp.exp(m - m_new)
        p = jnp.exp(s3 - m_new)
        l_new = alpha * l + jnp.sum(p, axis=-1, keepdims=True)
        p2 = p.reshape(rows, nk).astype(BF16)
        acc = alpha.reshape(rows, 1) * acc
        for i in range(ppc):
            acc = acc + _dot_nt(p2[:, i * PAGE_SIZE:(i + 1) * PAGE_SIZE], vbuf[slot, i].astype(BF16))
        return m_new, l_new, acc, run

    init = (jnp.full((heads, t_new, 1), MASKED, F32), jnp.zeros((heads, t_new, 1), F32),
            jnp.zeros((rows, FOX_NKV), F32), jnp.zeros((heads, 1), F32))
    m, l, acc, _ = lax.fori_loop(0, nchunks, body, init)

    kn = _pad_rows(kn_ref[...], LANES).astype(BF16)
    vn = _pad_rows(vn_ref[...], LANES).astype(BF16)
    s3 = _dot_nt(qbd, kn).reshape(heads, t_new, LANES) + dcol3 - dtp_ref[0][:, None, :]
    trow = lax.broadcasted_iota(jnp.int32, (t_new, LANES), 0)
    tcol = lax.broadcasted_iota(jnp.int32, (t_new, LANES), 1)
    s3 = jnp.where((tcol <= trow)[None], s3, MASKED)
    m, l, acc = _online_softmax_step(s3, vn, m, l, acc)
    o = acc / l.reshape(rows, 1)

    groups = []
    for kh in range(FOX_KV_HEADS):
        tot = jnp.zeros((t_new, FOX_NKV), F32)
        for gi in range(g):
            h = kh * g + gi
            blk = o[h * t_new:(h + 1) * t_new]
            shift = ((gi - kh) * hd) % FOX_NKV
            r = pltpu.roll(blk, shift, axis=1) if shift else blk
            tot = tot + jnp.where((lane >= gi * hd) & (lane < (gi + 1) * hd), r, 0.0)
        groups.append(tot)
    o_ref[...] = jnp.concatenate(groups, axis=1)


def _fox_sample_attention(q, k_new, v_new, ct, page_table, kt_pages, vt_pages, lf_pages, *, t_new):
    ns = q.shape[0]
    bd, n_pages = page_table.shape
    assert bd * t_new == ns and t_new == SUBLANES
    ppc = _tile(n_pages, 8)
    nchunks = n_pages // ppc
    rows = FOX_HEADS * t_new
    d_bht = ct.reshape(FOX_HEADS, bd, t_new).transpose(1, 0, 2)
    dcol = d_bht.reshape(bd, rows, 1)
    dtp = jnp.pad(d_bht, ((0, 0), (0, 0), (0, LANES - t_new)))
    anyspec = pl.BlockSpec(memory_space=pl.ANY)
    return pl.pallas_call(
        functools.partial(_fox_sample_kernel, ppc=ppc, nchunks=nchunks, t_new=t_new),
        grid_spec=pltpu.PrefetchScalarGridSpec(
            num_scalar_prefetch=1,
            grid=(bd,),
            in_specs=[
                pl.BlockSpec((t_new, FOX_NQ), lambda b, pt: (b, 0)),
                pl.BlockSpec((t_new, FOX_NKV), lambda b, pt: (b, 0)),
                pl.BlockSpec((t_new, FOX_NKV), lambda b, pt: (b, 0)),
                pl.BlockSpec((1, rows, 1), lambda b, pt: (b, 0, 0)),
                pl.BlockSpec((1, FOX_HEADS, LANES), lambda b, pt: (b, 0, 0)),
                anyspec, anyspec, anyspec,
            ],
            out_specs=pl.BlockSpec((t_new, FOX_NQ), lambda b, pt: (b, 0)),
            scratch_shapes=[pltpu.VMEM((2, ppc, FOX_NKV, PAGE_SIZE), F32),
                            pltpu.VMEM((2, ppc, FOX_NKV, PAGE_SIZE), F32),
                            pltpu.VMEM((2, ppc, FOX_HEADS, PAGE_SIZE), F32),
                            pltpu.SemaphoreType.DMA((2,)),
                            pltpu.SemaphoreType.DMA((2,)),
                            pltpu.SemaphoreType.DMA((2,))],
        ),
        out_shape=jax.ShapeDtypeStruct((ns, FOX_NQ), F32),
        compiler_params=_params(("arbitrary",)),
        name="fox_sample_attn",
    )(page_table, q, k_new, v_new, dcol, dtp, kt_pages, vt_pages, lf_pages)


def _oproj_kernel(o_ref, wo_ref, x_ref, g_ref, out_ref):
    y = _dot(o_ref[...].astype(BF16), wo_ref[...])
    out_ref[...] = x_ref[...] + _rms(y, g_ref[...])


def _out_project(o, w_o, x, g):
    n, d = x.shape
    e = o.shape[1]
    tm = _tile(n, 512)
    row = lambda i: (i, 0)
    const = lambda i: (0, 0)
    return pl.pallas_call(
        _oproj_kernel,
        grid=(n // tm,),
        in_specs=[pl.BlockSpec((tm, e), row), pl.BlockSpec((e, d), const),
                  pl.BlockSpec((tm, d), row), pl.BlockSpec((1, d), const)],
        out_specs=pl.BlockSpec((tm, d), row),
        out_shape=jax.ShapeDtypeStruct((n, d), F32),
        compiler_params=_params(("parallel",)),
        name="out_proj",
    )(o, w_o.astype(BF16), x, g.reshape(1, d))


def _mla_oproj_kernel(ol_ref, wuv_ref, wo_ref, x_ref, g_ref, out_ref):
    parts = []
    for j in range(MLA_HEADS // 2):
        pair = jnp.concatenate([ol_ref[2 * j], ol_ref[2 * j + 1]], axis=1).astype(BF16)
        parts.append(_dot(pair, wuv_ref[j]).astype(BF16))
    y = _dot(jnp.concatenate(parts, axis=1), wo_ref[...])
    out_ref[...] = x_ref[...] + _rms(y, g_ref[...])


def _mla_out_project(o_lat, w_uv, w_o, x, g):
    n, d = x.shape
    tm = _tile(n, 512)
    npair = MLA_HEADS // 2
    wp = w_uv.reshape(MLA_KV_LORA, npair, 2, MLA_V_DIM).transpose(1, 2, 0, 3)
    z = jnp.zeros_like(wp[:, 0])
    wbd = jnp.concatenate([jnp.concatenate([wp[:, 0], z], axis=2),
                           jnp.concatenate([z, wp[:, 1]], axis=2)], axis=1).astype(BF16)
    row = lambda i: (i, 0)
    const = lambda i: (0, 0)
    return pl.pallas_call(
        _mla_oproj_kernel,
        grid=(n // tm,),
        in_specs=[pl.BlockSpec((MLA_HEADS, tm, MLA_KV_LORA), lambda i: (0, i, 0)),
                  pl.BlockSpec(wbd.shape, lambda i: (0, 0, 0)),
                  pl.BlockSpec(w_o.shape, const),
                  pl.BlockSpec((tm, d), row), pl.BlockSpec((1, d), const)],
        out_specs=pl.BlockSpec((tm, d), row),
        out_shape=jax.ShapeDtypeStruct((n, d), F32),
        compiler_params=_params(("parallel",)),
        name="mla_out_proj",
    )(o_lat, wbd, w_o.astype(BF16), x, g.reshape(1, d))


def _mla_proj_kernel(x_ref, g_ref, wa_ref, qn_ref, kvn_ref, wnope_ref, wpe_ref, wper_ref, wuk_ref,
                     cos_ref, sin_ref, qp_ref, kvp_ref, ckv_ref, kpe_ref):
    tm = x_ref.shape[0]
    scale = (MLA_NOPE + MLA_ROPE) ** -0.5
    h = _rms(x_ref[...], g_ref[...]).astype(BF16)
    a = _dot(h, wa_ref[...])
    cos, sin = cos_ref[...], sin_ref[...]
    c_q = _rms(a[:, :MLA_Q_LORA], qn_ref[...]).astype(BF16)
    c_kv = _rms(a[:, MLA_Q_LORA:MLA_Q_LORA + MLA_KV_LORA], kvn_ref[...])
    r0 = MLA_Q_LORA + MLA_KV_LORA
    kpe_rep = a[:, r0:r0 + LANES] * cos + a[:, r0 + LANES:r0 + 2 * LANES] * sin
    ckv_ref[...] = c_kv
    kpe_ref[...] = kpe_rep[:, :MLA_ROPE]
    kvp_ref[...] = jnp.concatenate([c_kv, kpe_rep], axis=1).astype(BF16)

    nrep = MLA_HEADS // ROPE_REP
    cos_all = jnp.concatenate([cos] * nrep, axis=1)
    sin_all = jnp.concatenate([sin] * nrep, axis=1)
    qpe = (_dot(c_q, wpe_ref[...]) * cos_all + _dot(c_q, wper_ref[...]) * sin_all) * scale
    qnope = _dot(c_q, wnope_ref[...]).astype(BF16)
    lane = lax.broadcasted_iota(jnp.int32, (tm, LANES), 1)
    for j in range(MLA_HEADS // 2):
        qlat2 = _dot(qnope[:, j * LANES:(j + 1) * LANES], wuk_ref[j]) * scale
        for w in range(2):
            hh = 2 * j + w
            grp = qpe[:, (hh // ROPE_REP) * LANES:(hh // ROPE_REP + 1) * LANES]
            quarter = hh % ROPE_REP
            pe = jnp.where((lane >= quarter * MLA_ROPE) & (lane < (quarter + 1) * MLA_ROPE), grp, 0.0)
            qp_ref[hh] = jnp.concatenate([qlat2[:, w * LANES:(w + 1) * LANES], pe], axis=1).astype(qp_ref.dtype)


def _rot_half_cols(w):
    half = MLA_ROPE // 2
    return jnp.concatenate([-w[..., half:], w[..., :half]], axis=-1)


def _mla_project(x, g, w_a, q_norm, kv_norm, w_uq, w_uk, pos, *, q_dtype):
    n, d = x.shape
    tm = _tile(n, 256)
    npos = pos.shape[0]
    assert n % npos == 0 and npos % tm == 0
    grid = (n // npos, npos // tm)
    g1 = grid[1]

    half = MLA_ROPE // 2
    inv = ROPE_THETA ** (-jnp.arange(half, dtype=F32) / half)
    ang = pos.astype(F32)[:, None] * inv[None, :]
    cos = jnp.tile(jnp.cos(ang), (1, 2 * ROPE_REP))
    sin = jnp.tile(jnp.sin(ang), (1, 2 * ROPE_REP))

    r0 = MLA_Q_LORA + MLA_KV_LORA
    w_pe = w_a[:, r0:]
    wa_ext = jnp.concatenate([w_a[:, :r0], jnp.tile(w_pe, (1, ROPE_REP)),
                              jnp.tile(_rot_half_cols(w_pe), (1, ROPE_REP))], axis=1).astype(BF16)
    w_nope = w_uq[:, :, :MLA_NOPE].reshape(MLA_Q_LORA, MLA_HEADS * MLA_NOPE).astype(BF16)
    w_qpe = w_uq[:, :, MLA_NOPE:]
    w_pe_q = w_qpe.reshape(MLA_Q_LORA, MLA_HEADS * MLA_ROPE).astype(BF16)
    w_pe_qr = _rot_half_cols(w_qpe).reshape(MLA_Q_LORA, MLA_HEADS * MLA_ROPE).astype(BF16)
    npair = MLA_HEADS // 2
    wk = w_uk.reshape(MLA_KV_LORA, npair, 2, MLA_NOPE).transpose(1, 2, 3, 0)
    z = jnp.zeros_like(wk[:, 0])
    wuk_bd = jnp.concatenate([jnp.concatenate([wk[:, 0], z], axis=2),
                              jnp.concatenate([z, wk[:, 1]], axis=2)], axis=1).astype(BF16)

    tok = lambda i, j: (i * g1 + j, 0)
    const = lambda i, j: (0, 0)
    posb = lambda i, j: (j, 0)
    return pl.pallas_call(
        _mla_proj_kernel,
        grid=grid,
        in_specs=[
            pl.BlockSpec((tm, d), tok),
            pl.BlockSpec((1, d), const),
            pl.BlockSpec(wa_ext.shape, const),
            pl.BlockSpec((1, MLA_Q_LORA), const),
            pl.BlockSpec((1, MLA_KV_LORA), const),
            pl.BlockSpec(w_nope.shape, const),
            pl.BlockSpec(w_pe_q.shape, const),
            pl.BlockSpec(w_pe_qr.shape, const),
            pl.BlockSpec(wuk_bd.shape, lambda i, j: (0, 0, 0)),
            pl.BlockSpec((tm, LANES), posb),
            pl.BlockSpec((tm, LANES), posb),
        ],
        out_specs=[
            pl.BlockSpec((MLA_HEADS, tm, MLA_QK), lambda i, j: (0, i * g1 + j, 0)),
            pl.BlockSpec((tm, MLA_QK), tok),
            pl.BlockSpec((tm, MLA_KV_LORA), tok),
            pl.BlockSpec((tm, MLA_ROPE), tok),
        ],
        out_shape=[
            jax.ShapeDtypeStruct((MLA_HEADS, n, MLA_QK), q_dtype),
            jax.ShapeDtypeStruct((n, MLA_QK), BF16),
            jax.ShapeDtypeStruct((n, MLA_KV_LORA), F32),
            jax.ShapeDtypeStruct((n, MLA_ROPE), F32),
        ],
        compiler_params=_params(("parallel", "parallel")),
        name="mla_proj",
    )(x, g.reshape(1, d), wa_ext, q_norm.reshape(1, -1), kv_norm.reshape(1, -1),
      w_nope, w_pe_q, w_pe_qr, wuk_bd, cos, sin)


def _mla_attn_kernel(q_ref, kv_ref, o_ref, *, tq, tk):
    qi = pl.program_id(1)
    nh = MLA_HEADS
    q = q_ref[...].reshape(nh * tq, MLA_QK)
    nfull = (qi * tq) // tk

    def step(j, carry, masked):
        off = pl.multiple_of(j * tk, tk)
        kv = kv_ref[0, pl.ds(off, tk), :]
        s3 = _dot_nt(q, kv).reshape(nh, tq, tk)
        if masked:
            row = qi * tq + lax.broadcasted_iota(jnp.int32, (tq, tk), 0)
            col = j * tk + lax.broadcasted_iota(jnp.int32, (tq, tk), 1)
            s3 = jnp.where((col <= row)[None], s3, MASKED)
        return _online_softmax_step(s3, kv[:, :MLA_KV_LORA], *carry)

    init = (jnp.full((nh, tq, 1), MASKED, F32), jnp.zeros((nh, tq, 1), F32),
            jnp.zeros((nh * tq, MLA_KV_LORA), F32))
    carry = lax.fori_loop(0, nfull, lambda j, c: step(j, c, False), init)
    _, l, acc = step(nfull, carry, True)
    o_ref[...] = (acc / l.reshape(nh * tq, 1)).reshape(nh, tq, MLA_KV_LORA).astype(o_ref.dtype)


def _mla_prompt_attention(qp, kvp, *, batch, seq_len):
    n = batch * seq_len
    tq = _tile(seq_len, 64)
    tk = _tile(seq_len, 256)
    assert tk % tq == 0
    nq = seq_len // tq
    kv3 = kvp.reshape(batch, seq_len, MLA_QK)
    return pl.pallas_call(
        functools.partial(_mla_attn_kernel, tq=tq, tk=tk),
        grid=(batch, nq),
        in_specs=[pl.BlockSpec((MLA_HEADS, tq, MLA_QK), lambda b, i: (0, b * nq + i, 0)),
                  pl.BlockSpec((1, seq_len, MLA_QK), lambda b, i: (b, 0, 0))],
        out_specs=pl.BlockSpec((MLA_HEADS, tq, MLA_KV_LORA), lambda b, i: (0, b * nq + i, 0)),
        out_shape=jax.ShapeDtypeStruct((MLA_HEADS, n, MLA_KV_LORA), BF16),
        compiler_params=_params(("parallel", "arbitrary")),
        name="mla_prompt_attn",
    )(qp, kv3)


def _mla_sample_kernel(pt_ref, q_ref, cn_ref, rn_ref, c_hbm, r_hbm, o_ref, cbuf, rbuf, csem, rsem,
                       *, ppc, nchunks, t_new):
    b = pl.program_id(0)
    nk = ppc * PAGE_SIZE
    nh = MLA_HEADS
    rows = nh * t_new
    pairs = ((c_hbm, cbuf, csem), (r_hbm, rbuf, rsem))

    for cp in _page_copies(pt_ref, b, 0, 0, ppc, pairs):
        cp.start()

    q = q_ref[...].reshape(rows, MLA_QK)
    qlat = q[:, :MLA_KV_LORA].astype(BF16)
    grp = q[:, MLA_KV_LORA:]
    qpe = grp[:, :MLA_ROPE]
    for i in range(1, ROPE_REP):
        qpe = qpe + grp[:, i * MLA_ROPE:(i + 1) * MLA_ROPE]
    qpe = qpe.astype(BF16)

    def body(c, carry):
        slot = c % 2

        @pl.when(c + 1 < nchunks)
        def _():
            for cp in _page_copies(pt_ref, b, c + 1, 1 - slot, ppc, pairs):
                cp.start()

        for cp in _page_copies(pt_ref, b, c, slot, ppc, pairs):
            cp.wait()
        ckv = cbuf[slot].reshape(nk, MLA_KV_LORA).astype(BF16)
        s_pe = jnp.concatenate([_dot(qpe, rbuf[slot, i].astype(BF16)) for i in range(ppc)], axis=1)
        s3 = (_dot_nt(qlat, ckv) + s_pe).reshape(nh, t_new, nk)
        return _online_softmax_step(s3, ckv, *carry)

    init = (jnp.full((nh, t_new, 1), MASKED, F32), jnp.zeros((nh, t_new, 1), F32),
            jnp.zeros((rows, MLA_KV_LORA), F32))
    m, l, acc = lax.fori_loop(0, nchunks, body, init)

    cn = _pad_rows(cn_ref[...], LANES).astype(BF16)
    rn = _pad_rows(rn_ref[...], LANES).astype(BF16)
    s3 = (_dot_nt(qlat, cn) + _dot_nt(qpe, rn)).reshape(nh, t_new, LANES)
    trow = lax.broadcasted_iota(jnp.int32, (t_new, LANES), 0)
    tcol = lax.broadcasted_iota(jnp.int32, (t_new, LANES), 1)
    s3 = jnp.where((tcol <= trow)[None], s3, MASKED)
    m, l, acc = _online_softmax_step(s3, cn, m, l, acc)
    o_ref[...] = (acc / l.reshape(rows, 1)).reshape(nh, t_new, MLA_KV_LORA)


def _mla_sample_attention(qp, c_new, kpe_new, page_table, ckv_pages, kpet_pages, *, t_new):
    ns = c_new.shape[0]
    bd, n_pages = page_table.shape
    assert bd * t_new == ns and t_new == SUBLANES
    ppc = _tile(n_pages, 16)
    nchunks = n_pages // ppc
    return pl.pallas_call(
        functools.partial(_mla_sample_kernel, ppc=ppc, nchunks=nchunks, t_new=t_new),
        grid_spec=pltpu.PrefetchScalarGridSpec(
            num_scalar_prefetch=1,
            grid=(bd,),
            in_specs=[
                pl.BlockSpec((MLA_HEADS, t_new, MLA_QK), lambda b, pt: (0, b, 0)),
                pl.BlockSpec((t_new, MLA_KV_LORA), lambda b, pt: (b, 0)),
                pl.BlockSpec((t_new, MLA_ROPE), lambda b, pt: (b, 0)),
                pl.BlockSpec(memory_space=pl.ANY),
                pl.BlockSpec(memory_space=pl.ANY),
            ],
            out_specs=pl.BlockSpec((MLA_HEADS, t_new, MLA_KV_LORA), lambda b, pt: (0, b, 0)),
            scratch_shapes=[pltpu.VMEM((2, ppc, PAGE_SIZE, MLA_KV_LORA), F32),
                            pltpu.VMEM((2, ppc, MLA_ROPE, PAGE_SIZE), F32),
                            pltpu.SemaphoreType.DMA((2,)),
                            pltpu.SemaphoreType.DMA((2,))],
        ),
        out_shape=jax.ShapeDtypeStruct((MLA_HEADS, ns, MLA_KV_LORA), F32),
        compiler_params=_params(("arbitrary",)),
        name="mla_sample_attn",
    )(page_table, qp, c_new, kpe_new, ckv_pages, kpet_pages)


def _ffn_kernel(x_ref, g1_ref, g2_ref, win_ref, cw_ref, cb_ref, wout_ref, *rest, fc, seq_rows, chained):
    if chained:
        out_ref, st_out_ref, act_ref, carry_ref = rest
    else:
        st_ref, out_ref, st_out_ref, act_ref = rest
    tm, d = x_ref.shape
    dff = wout_ref.shape[0]
    x = x_ref[...]
    h = _rms(x, g1_ref[...]).astype(BF16)

    if chained:
        @pl.when(pl.program_id(1) == 0)
        def _():
            carry_ref[...] = jnp.zeros_like(carry_ref)
        row = lax.broadcasted_iota(jnp.int32, (tm, fc), 0)
    else:
        nseq = tm // seq_rows
        row = lax.broadcasted_iota(jnp.int32, (nseq, seq_rows, fc), 1)

    for f in range(dff // fc):
        cs = slice(f * fc, (f + 1) * fc)
        gate = _dot(h, win_ref[:, cs])
        up = _dot(h, win_ref[:, dff + f * fc:dff + (f + 1) * fc])
        w0, w1, w2 = cw_ref[0:1, cs], cw_ref[1:2, cs], cw_ref[2:3, cs]
        if chained:
            p0, p1 = carry_ref[0:1, cs], carry_ref[1:2, cs]
            sh1 = jnp.where(row == 0, p1, pltpu.roll(gate, 1, axis=0))
            sh2 = jnp.where(row == 0, p0, jnp.where(row == 1, p1, pltpu.roll(gate, 2, axis=0)))
            carry_ref[0:2, cs] = gate[tm - 2:tm]
            st_out_ref[0, :, cs] = gate[tm - 2:tm]
            conv = cb_ref[:, cs] + w0 * sh2 + w1 * sh1 + w2 * gate
        else:
            g3 = gate.reshape(nseq, seq_rows, fc)
            p0, p1 = st_ref[:, 0:1, cs], st_ref[:, 1:2, cs]
            sh1 = jnp.where(row == 0, p1, pltpu.roll(g3, 1, axis=1))
            sh2 = jnp.where(row == 0, p0, jnp.where(row == 1, p1, pltpu.roll(g3, 2, axis=1)))
            st_out_ref[:, :, cs] = g3[:, seq_rows - 2:seq_rows, :]
            conv = (cb_ref[:, cs] + w0 * sh2 + w1 * sh1 + w2 * g3).reshape(tm, fc)
        act_ref[:, cs] = (jax.nn.gelu(conv, approximate=True) * up).astype(BF16)

    y = _dot(act_ref[...], wout_ref[...])
    out_ref[...] = x + _rms(y, g2_ref[...])


def _conv_ffn_block(x, g1, g2, w_in, conv_w, conv_b, w_out, state, *, seq_len):
    n, d = x.shape
    dff = w_out.shape[0]
    fc = 256
    assert dff % fc == 0 and CONV_W == 3
    nseq_total = n // seq_len
    chained = state is None
    wspec = dict(pipeline_mode=pl.Buffered(1))
    if chained:
        tm = _tile(seq_len, 512)
        grid = (nseq_total, seq_len // tm)
        g1n = grid[1]
        tok = lambda i, j: (i * g1n + j, 0)
        const = lambda i, j: (0, 0)
        st_spec = pl.BlockSpec((1, CONV_W - 1, dff), lambda i, j: (i, 0, 0))
        extra_in, extra_specs = [], []
        scratch = [pltpu.VMEM((tm, dff), BF16), pltpu.VMEM((SUBLANES, dff), F32)]
        sem = ("arbitrary", "arbitrary")
    else:
        sb = _tile(nseq_total, 32)
        tm = sb * seq_len
        grid = (nseq_total // sb,)
        tok = lambda i: (i, 0)
        const = lambda i: (0, 0)
        st_spec = pl.BlockSpec((sb, CONV_W - 1, dff), lambda i: (i, 0, 0))
        extra_in, extra_specs = [state], [st_spec]
        scratch = [pltpu.VMEM((tm, dff), BF16)]
        sem = ("arbitrary",)
    out, st_new = pl.pallas_call(
        functools.partial(_ffn_kernel, fc=fc, seq_rows=seq_len, chained=chained),
        grid=grid,
        in_specs=[
            pl.BlockSpec((tm, d), tok),
            pl.BlockSpec((1, d), const),
            pl.BlockSpec((1, d), const),
            pl.BlockSpec((d, 2 * dff), const, **wspec),
            pl.BlockSpec((CONV_W, dff), const),
            pl.BlockSpec((1, dff), const),
            pl.BlockSpec((dff, d), const, **wspec),
        ] + extra_specs,
        out_specs=[pl.BlockSpec((tm, d), tok), st_spec],
        out_shape=[jax.ShapeDtypeStruct((n, d), F32),
                   jax.ShapeDtypeStruct((nseq_total, CONV_W - 1, dff), F32)],
        scratch_shapes=scratch,
        compiler_params=_params(sem),
        name="conv_ffn",
    )(x, g1.reshape(1, d), g2.reshape(1, d), w_in.astype(BF16), conv_w, conv_b.reshape(1, dff),
      w_out.astype(BF16), *extra_in)
    return out, st_new


def kernel(x_prompt, x_sample, cache_fox_k, cache_fox_v, cache_fox_logf, cache_mla_ckv, cache_mla_kpe, state_conv, page_table, norm_mix_pre, norm_mix_post, norm_ffn_pre, norm_ffn_post, fox_w_in, fox_b_f, fox_w_o, mla_w_a, mla_q_norm, mla_kv_norm, mla_w_uq, mla_w_uk, mla_w_uv, mla_w_o, ffn_w_in, ffn_conv_w, ffn_conv_b, ffn_w_out):
    batch, seq_len, d = x_prompt.shape
    bd, t_new, _ = x_sample.shape
    depth = norm_mix_pre.shape[0]
    n_pool = cache_fox_k.shape[1]
    past = page_table.shape[1] * PAGE_SIZE
    xp = x_prompt.reshape(batch * seq_len, d)
    xs = x_sample.reshape(bd * t_new, d)

    fox_kt_pages = jnp.transpose(cache_fox_k, (0, 1, 3, 4, 2)).reshape(-1, FOX_NKV, PAGE_SIZE)
    fox_vt_pages = jnp.transpose(cache_fox_v, (0, 1, 3, 4, 2)).reshape(-1, FOX_NKV, PAGE_SIZE)
    fox_f_pages = jnp.transpose(cache_fox_logf, (0, 1, 3, 2)).reshape(-1, FOX_HEADS, PAGE_SIZE)
    mla_c_pages = cache_mla_ckv.reshape(-1, PAGE_SIZE, MLA_KV_LORA)
    mla_rt_pages = jnp.transpose(cache_mla_kpe, (0, 1, 3, 2)).reshape(-1, MLA_ROPE, PAGE_SIZE)

    outs_p = {k: [] for k in ("fk", "fv", "ff", "mc", "mr", "cv")}
    outs_s = {k: [] for k in ("fk", "fv", "ff", "mc", "mr", "cv")}
    for i in range(depth):
        li = i // N_MIXERS
        pages = page_table + li * n_pool
        if i % N_MIXERS == 0:
            w_in, b_f, w_o = fox_w_in[li], fox_b_f[li], fox_w_o[li]
            q, k, v, kbf, vbf, logf, c, ct = _fox_project(xp, norm_mix_pre[i], w_in, b_f, seq_len=seq_len, q_dtype=BF16)
            o = _fox_prompt_attention(q, kbf, vbf, c, ct, batch=batch, seq_len=seq_len)
            xp = _out_project(o, w_o, xp, norm_mix_post[i])
            outs_p["fk"].append(k.reshape(batch, seq_len, FOX_KV_HEADS, FOX_HEAD_DIM))
            outs_p["fv"].append(v.reshape(batch, seq_len, FOX_KV_HEADS, FOX_HEAD_DIM))
            outs_p["ff"].append(logf.reshape(batch, seq_len, FOX_HEADS))

            q, k, v, _, _, logf, _, ct = _fox_project(xs, norm_mix_pre[i], w_in, b_f, seq_len=t_new, q_dtype=F32)
            o = _fox_sample_attention(q, k, v, ct, pages, fox_kt_pages, fox_vt_pages, fox_f_pages, t_new=t_new)
            xs = _out_project(o, w_o, xs, norm_mix_post[i])
            outs_s["fk"].append(k.reshape(bd, t_new, FOX_KV_HEADS, FOX_HEAD_DIM))
            outs_s["fv"].append(v.reshape(bd, t_new, FOX_KV_HEADS, FOX_HEAD_DIM))
            outs_s["ff"].append(logf.reshape(bd, t_new, FOX_HEADS))
        else:
            args = (mla_w_a[li], mla_q_norm[li], mla_kv_norm[li], mla_w_uq[li], mla_w_uk[li])
            qp, kvp, ckv, kpe = _mla_project(xp, norm_mix_pre[i], *args, jnp.arange(seq_len), q_dtype=BF16)
            o_lat = _mla_prompt_attention(qp, kvp, batch=batch, seq_len=seq_len)
            xp = _mla_out_project(o_lat, mla_w_uv[li], mla_w_o[li], xp, norm_mix_post[i])
            outs_p["mc"].append(ckv.reshape(batch, seq_len, MLA_KV_LORA))
            outs_p["mr"].append(kpe.reshape(batch, seq_len, MLA_ROPE))

            pos_s = jnp.tile(past + jnp.arange(t_new), bd)
            qp, _, ckv, kpe = _mla_project(xs, norm_mix_pre[i], *args, pos_s, q_dtype=F32)
            o_lat = _mla_sample_attention(qp, ckv, kpe, pages, mla_c_pages, mla_rt_pages, t_new=t_new)
            xs = _mla_out_project(o_lat, mla_w_uv[li], mla_w_o[li], xs, norm_mix_post[i])
            outs_s["mc"].append(ckv.reshape(bd, t_new, MLA_KV_LORA))
            outs_s["mr"].append(kpe.reshape(bd, t_new, MLA_ROPE))

        ffn = (norm_ffn_pre[i], norm_ffn_post[i], ffn_w_in[i], ffn_conv_w[i], ffn_conv_b[i], ffn_w_out[i])
        xp, cvp = _conv_ffn_block(xp, *ffn, None, seq_len=seq_len)
        xs, cvs = _conv_ffn_block(xs, *ffn, state_conv[i], seq_len=t_new)
        outs_p["cv"].append(cvp)
        outs_s["cv"].append(cvs)

    ldt = cache_fox_logf.dtype
    st = jnp.stack
    return (xp.reshape(batch, seq_len, d), xs.reshape(bd, t_new, d),
            st(outs_p["fk"]), st(outs_p["fv"]), st(outs_p["ff"]).astype(ldt), st(outs_p["mc"]), st(outs_p["mr"]), st(outs_p["cv"]),
            st(outs_s["fk"]), st(outs_s["fv"]), st(outs_s["ff"]).astype(ldt), st(outs_s["mc"]), st(outs_s["mr"]), st(outs_s["cv"]))
```

```python
import functools

import jax
import jax.numpy as jnp
from jax import lax
from jax.experimental import pallas as pl
from jax.experimental.pallas import tpu as pltpu

F32 = jnp.float32
BF16 = jnp.bfloat16

N_MIXERS = 2
FOX_HEADS = 16
FOX_KV_HEADS = 4
FOX_GROUP = FOX_HEADS // FOX_KV_HEADS
FOX_HEAD_DIM = 64
FOX_NQ = FOX_HEADS * FOX_HEAD_DIM
FOX_NKV = FOX_KV_HEADS * FOX_HEAD_DIM
MLA_HEADS = 16
MLA_Q_LORA = 256
MLA_KV_LORA = 128
MLA_NOPE = 64
MLA_ROPE = 32
MLA_V_DIM = 64
ROPE_THETA = 10000.0
CONV_W = 3
PAGE_SIZE = 128
EPS = 1e-6

LANES = 128
SUBLANES = 8
V7X_VMEM_LIMIT_BYTES = 56 * 1024 * 1024

MASKED = -1e30
MLA_QK = 2 * LANES
ROPE_REP = LANES // MLA_ROPE

LOG2E = 1.4426950408889634
ONES_ROWS = 16
FOX_BIAS_PIECES = 3
FOX_BIAS_LANE0 = FOX_HEAD_DIM

NT_DIMS = (((1,), (1,)), ((), ()))


def _params(sem):
    return pltpu.CompilerParams(dimension_semantics=sem, vmem_limit_bytes=V7X_VMEM_LIMIT_BYTES)


def _tile(n, pref):
    t = min(n, pref)
    assert n % t == 0, (n, pref)
    return t


def _rms(x, g):
    return x * lax.rsqrt(jnp.mean(x * x, axis=-1, keepdims=True) + EPS) * g


def _log_sigmoid(x):
    return jnp.minimum(x, 0.0) - jnp.log1p(jnp.exp(-jnp.abs(x)))


def _split3(x):
    hi = x.astype(BF16)
    r1 = x - hi.astype(F32)
    mid = r1.astype(BF16)
    lo = (r1 - mid.astype(F32)).astype(BF16)
    return hi, mid, lo


def _dot(a, b):
    return jnp.dot(a, b, preferred_element_type=F32)


def _dot_nt(a, b):
    return lax.dot_general(a, b, NT_DIMS, preferred_element_type=F32)


def _fox_proj_kernel(x_ref, g_ref, w_ref, wft_ref, bf_ref, bft_ref, u_ref, l_ref, pa_ref, pb_ref,
                     q_ref, k_ref, v_ref, ka_ref, kb_ref, vt_ref, logf_ref, ct_ref,
                     carry_ref, carryt_ref, *, logit_scale):
    tm = x_ref.shape[0]
    hd = FOX_HEAD_DIM

    @pl.when(pl.program_id(1) == 0)
    def _():
        carry_ref[...] = jnp.zeros_like(carry_ref)
        carryt_ref[...] = jnp.zeros_like(carryt_ref)

    h = _rms(x_ref[...], g_ref[...]).astype(BF16)
    proj = _dot(h, w_ref[...])
    q_ref[...] = (proj[:, :FOX_NQ] * (hd ** -0.5 * logit_scale)).astype(q_ref.dtype)
    k = proj[:, FOX_NQ:FOX_NQ + FOX_NKV]
    v = proj[:, FOX_NQ + FOX_NKV:FOX_NQ + 2 * FOX_NKV]
    k_ref[...] = k
    v_ref[...] = v
    vt = v.T.astype(BF16)
    for kh in range(FOX_KV_HEADS):
        vt_ref[kh, 0:hd, :] = vt[kh * hd:(kh + 1) * hd]
        vt_ref[kh, hd:hd + ONES_ROWS, :] = jnp.ones((ONES_ROWS, tm), BF16)
    f0 = FOX_NQ + 2 * FOX_NKV
    logf = _log_sigmoid(proj[:, f0:f0 + FOX_HEADS] + bf_ref[...])
    logf_ref[...] = logf
    lmat = l_ref[...]
    c = carry_ref[...]
    for piece in _split3(logf):
        c = c + _dot(lmat, piece)
    carry_ref[...] = c[tm - 1:tm, :]
    placed_a = jnp.zeros((tm, FOX_KV_HEADS * LANES), F32)
    placed_b = jnp.zeros((tm, FOX_KV_HEADS * LANES), F32)
    for p, piece in enumerate(_split3(c * logit_scale)):
        placed_a = placed_a + _dot(piece, pa_ref[p])
        placed_b = placed_b + _dot(piece, pb_ref[p])
    lane = lax.broadcasted_iota(jnp.int32, (tm, LANES), 1)
    for kh in range(FOX_KV_HEADS):
        grp = k[:, (kh // 2) * LANES:(kh // 2 + 1) * LANES]
        swapped = pltpu.roll(grp, hd, axis=1)
        low, high = (grp, swapped) if kh % 2 == 0 else (swapped, grp)
        sl = slice(kh * LANES, (kh + 1) * LANES)
        ka_ref[kh] = jnp.where(lane < hd, low, placed_a[:, sl]).astype(BF16)
        kb_ref[kh] = jnp.where(lane >= hd, high, placed_b[:, sl]).astype(BF16)
    logft = _log_sigmoid(_dot_nt(wft_ref[...], h) + bft_ref[...])
    umat = u_ref[...]
    ct = carryt_ref[...]
    for piece in _split3(logft):
        ct = ct + _dot(piece, umat)
    ct_ref[...] = ct
    carryt_ref[...] = ct[:, tm - 1:tm]


def _fox_project(x, g, w_in, b_f, *, seq_len, q_dtype, logit_scale):
    n, d = x.shape
    tm = _tile(n, 512)
    if seq_len >= tm:
        assert seq_len % tm == 0
        grid = (n // seq_len, seq_len // tm)
        grp = jnp.zeros((tm,), jnp.int32)
    else:
        assert tm % seq_len == 0
        grid = (n // tm, 1)
        grp = jnp.arange(tm, dtype=jnp.int32) // seq_len
    g1 = grid[1]
    pos = jnp.arange(tm, dtype=jnp.int32)
    same = grp[:, None] == grp[None, :]
    umat = (same & (pos[:, None] <= pos[None, :])).astype(BF16)
    lmat = umat.T
    f0 = FOX_NQ + 2 * FOX_NKV
    w_main = jnp.pad(w_in, ((0, 0), (0, LANES - FOX_HEADS))).astype(BF16)
    wft = w_in[:, f0:].T.astype(BF16)
    wcols = w_main.shape[1]
    hh = jnp.arange(FOX_HEADS)
    pp = jnp.arange(FOX_BIAS_PIECES)
    col_b = (hh // FOX_GROUP)[None, :] * LANES + FOX_BIAS_PIECES * (hh % FOX_GROUP)[None, :] + pp[:, None]
    cols = jnp.arange(FOX_KV_HEADS * LANES)
    place_b = (col_b[:, :, None] == cols[None, None, :]).astype(BF16)
    place_a = (col_b[:, :, None] + FOX_HEAD_DIM == cols[None, None, :]).astype(BF16)

    tok = lambda i, j: (i * g1 + j, 0)
    tok3 = lambda i, j: (0, i * g1 + j, 0)
    tokt = lambda i, j: (0, i * g1 + j)
    const = lambda i, j: (0, 0)
    const3 = lambda i, j: (0, 0, 0)
    outs = pl.pallas_call(
        functools.partial(_fox_proj_kernel, logit_scale=logit_scale),
        grid=grid,
        in_specs=[
            pl.BlockSpec((tm, d), tok),
            pl.BlockSpec((1, d), const),
            pl.BlockSpec((d, wcols), const),
            pl.BlockSpec((FOX_HEADS, d), const),
            pl.BlockSpec((1, FOX_HEADS), const),
            pl.BlockSpec((FOX_HEADS, 1), const),
            pl.BlockSpec((tm, tm), const),
            pl.BlockSpec((tm, tm), const),
            pl.BlockSpec(place_a.shape, const3),
            pl.BlockSpec(place_b.shape, const3),
        ],
        out_specs=[
            pl.BlockSpec((tm, FOX_NQ), tok),
            pl.BlockSpec((tm, FOX_NKV), tok),
            pl.BlockSpec((tm, FOX_NKV), tok),
            pl.BlockSpec((FOX_KV_HEADS, tm, LANES), tok3),
            pl.BlockSpec((FOX_KV_HEADS, tm, LANES), tok3),
            pl.BlockSpec((FOX_KV_HEADS, FOX_HEAD_DIM + ONES_ROWS, tm), lambda i, j: (0, 0, i * g1 + j)),
            pl.BlockSpec((tm, FOX_HEADS), tok),
            pl.BlockSpec((FOX_HEADS, tm), tokt),
        ],
        out_shape=[
            jax.ShapeDtypeStruct((n, FOX_NQ), q_dtype),
            jax.ShapeDtypeStruct((n, FOX_NKV), F32),
            jax.ShapeDtypeStruct((n, FOX_NKV), F32),
            jax.ShapeDtypeStruct((FOX_KV_HEADS, n, LANES), BF16),
            jax.ShapeDtypeStruct((FOX_KV_HEADS, n, LANES), BF16),
            jax.ShapeDtypeStruct((FOX_KV_HEADS, FOX_HEAD_DIM + ONES_ROWS, n), BF16),
            jax.ShapeDtypeStruct((n, FOX_HEADS), F32),
            jax.ShapeDtypeStruct((FOX_HEADS, n), F32),
        ],
        scratch_shapes=[pltpu.VMEM((1, FOX_HEADS), F32), pltpu.VMEM((FOX_HEADS, 1), F32)],
        compiler_params=_params(("arbitrary", "arbitrary")),
        name="fox_proj",
    )(x, g.reshape(1, d), w_main, wft, b_f.reshape(1, FOX_HEADS), b_f.reshape(FOX_HEADS, 1), umat, lmat,
      place_a, place_b)
    return outs


def _online_softmax_step(s, v, m, l, acc):
    rows, n = acc.shape[0], s.shape[-1]
    m_new = jnp.maximum(m, jnp.max(s, axis=-1, keepdims=True))
    alpha = jnp.exp(m - m_new)
    p = jnp.exp(s - m_new)
    l_new = alpha * l + jnp.sum(p, axis=-1, keepdims=True)
    pv = _dot(p.reshape(rows, n).astype(BF16), v)
    return m_new, l_new, alpha.reshape(rows, 1) * acc + pv


def _softmax_steps_t(sts, vt1, states, cqs):
    probs = []
    for st, (m, _), cq in zip(sts, states, cqs):
        mt = jnp.max(st, axis=0, keepdims=True)
        if cq is not None:
            mt = mt + cq
        m_new = jnp.maximum(m, mt)
        p = jnp.exp2(st - (m_new if cq is None else m_new - cq))
        probs.append((m_new, jnp.exp2(m - m_new), p.astype(BF16)))
    return tuple((m_new, alpha * acc + _dot(vt1, p)) for (m_new, alpha, p), (_, acc) in zip(probs, states))


def _normalised_t(acc, dv):
    return acc[:dv] / acc[dv:dv + 1]


def _causal_mask_t(st, key0, row0, tq):
    key = key0 + lax.broadcasted_iota(jnp.int32, st.shape, 0)
    row = row0 + lax.rem(lax.broadcasted_iota(jnp.int32, st.shape, 1), tq)
    return jnp.where(key <= row, st, MASKED)


def _fox_attn_kernel(q_ref, ka_ref, kb_ref, vt_ref, ct_ref, o_ref, *, tq, tk):
    qi = pl.program_id(2)
    hd = FOX_HEAD_DIM
    q = q_ref[0]
    cq = ct_ref[0]
    lane = lax.broadcasted_iota(jnp.int32, (tq, LANES), 1)

    def q_operand(gi):
        grp = q[:, (gi // 2) * LANES:(gi // 2 + 1) * LANES]
        base = FOX_BIAS_PIECES * gi + (FOX_BIAS_LANE0 if gi % 2 == 0 else 0)
        minus = jnp.where((lane >= base) & (lane < base + FOX_BIAS_PIECES), -1.0, 0.0).astype(q.dtype)
        return jnp.where((lane < hd) if gi % 2 == 0 else (lane >= hd), grp, minus)

    heads = [((ka_ref, kb_ref)[gi % 2], q_operand(gi), cq[gi:gi + 1] * LOG2E) for gi in range(FOX_GROUP)]

    def logits(j):
        off = pl.multiple_of(j * tk, tk)
        return [_dot_nt(k_ref[0, 0, pl.ds(off, tk), :], qx) for k_ref, qx, _ in heads]

    def consume(j, sts, states, masked):
        off = pl.multiple_of(j * tk, tk)
        if masked:
            sts = [_causal_mask_t(st, off, qi * tq, tq) for st in sts]
        return _softmax_steps_t(sts, vt_ref[0, :, pl.ds(off, tk)], states, [cqx for _, _, cqx in heads])

    init = tuple((jnp.full((1, tq), MASKED, F32), jnp.zeros((hd + ONES_ROWS, tq), F32)) for _ in heads)
    nfull = (qi * tq) // tk
    states = lax.fori_loop(0, nfull, lambda j, s: consume(j, logits(j), s, False), init)
    ot = jnp.concatenate([_normalised_t(acc, hd) for _, acc in consume(nfull, logits(nfull), states, True)],
                         axis=0)
    o_ref[0] = ot.T.astype(o_ref.dtype)


def _fox_prompt_attention(q, ka, kb, vt, ct, *, batch, seq_len):
    n = batch * seq_len
    tq = _tile(seq_len, 256)
    tk = _tile(seq_len, 512)
    assert tk % tq == 0 and FOX_GROUP == 4
    nq = seq_len // tq
    gw = FOX_GROUP * FOX_HEAD_DIM
    q3 = q.reshape(batch, seq_len, FOX_NQ)
    ka4 = ka.reshape(FOX_KV_HEADS, batch, seq_len, LANES)
    kb4 = kb.reshape(FOX_KV_HEADS, batch, seq_len, LANES)
    ct3 = ct.reshape(FOX_KV_HEADS, FOX_GROUP, n)
    kspec = pl.BlockSpec((1, 1, seq_len, LANES), lambda b, h, i: (h, b, 0, 0))
    o = pl.pallas_call(
        functools.partial(_fox_attn_kernel, tq=tq, tk=tk),
        grid=(batch, FOX_KV_HEADS, nq),
        in_specs=[
            pl.BlockSpec((1, tq, gw), lambda b, h, i: (b, i, h)),
            kspec, kspec,
            pl.BlockSpec((1, FOX_HEAD_DIM + ONES_ROWS, seq_len), lambda b, h, i: (h, 0, b)),
            pl.BlockSpec((1, FOX_GROUP, tq), lambda b, h, i: (h, 0, b * nq + i)),
        ],
        out_specs=pl.BlockSpec((1, tq, gw), lambda b, h, i: (b, i, h)),
        out_shape=jax.ShapeDtypeStruct((batch, seq_len, FOX_NQ), BF16),
        compiler_params=_params(("parallel", "parallel", "arbitrary")),
        name="fox_prompt_attn",
    )(q3, ka4, kb4, vt, ct3)
    return o.reshape(n, FOX_NQ)


def _page_copies(pt_ref, b, chunk, slot, ppc, pairs):
    out = []
    for i in range(ppc):
        page = pt_ref[b, chunk * ppc + i]
        for hbm, buf, sem in pairs:
            out.append(pltpu.make_async_copy(hbm.at[page], buf.at[slot, i], sem.at[slot]))
    return out


def _paged_chunk_step(pt_ref, ci, nchunks, ppc, pairs, reverse):
    b, nb = pl.program_id(0), pl.num_programs(0)
    order = (lambda c: nchunks - 1 - c) if reverse else (lambda c: c)
    slot = (b * nchunks + ci) % 2
    last = ci + 1 == nchunks
    nxt_b = jnp.where(last, b + 1, b)
    nxt_chunk = order(jnp.where(last, 0, ci + 1))

    @pl.when(jnp.logical_and(b == 0, ci == 0))
    def _():
        for cp in _page_copies(pt_ref, b, order(ci), slot, ppc, pairs):
            cp.start()

    @pl.when(jnp.logical_or(ci + 1 < nchunks, b + 1 < nb))
    def _():
        for cp in _page_copies(pt_ref, nxt_b, nxt_chunk, 1 - slot, ppc, pairs):
            cp.start()

    for cp in _page_copies(pt_ref, b, order(ci), slot, ppc, pairs):
        cp.wait()
    return order(ci), slot


def _pad_rows(x, rows):
    return jnp.concatenate([x, jnp.zeros((rows - x.shape[0], x.shape[1]), x.dtype)], axis=0)


def _suffix_sum_lanes(x):
    lane = lax.broadcasted_iota(jnp.int32, x.shape, 1)
    y = x
    sh = 1
    while sh < LANES:
        y = y + jnp.where(lane < LANES - sh, pltpu.roll(y, LANES - sh, axis=1), 0.0)
        sh *= 2
    return y


def _fox_sample_kernel(pt_ref, q_ref, kn_ref, vn_ref, dcol_ref, dtp_ref, kt_hbm, vt_hbm, lf_hbm,
                       o_ref, kbuf, vbuf, lbuf, ksem, vsem, lsem, *, ppc, nchunks, t_new):
    nk = ppc * PAGE_SIZE
    hd, g, heads = FOX_HEAD_DIM, FOX_GROUP, FOX_HEADS
    rows = heads * t_new
    pairs = ((kt_hbm, kbuf, ksem), (vt_hbm, vbuf, vsem), (lf_hbm, lbuf, lsem))

    q = q_ref[...]
    lane = lax.broadcasted_iota(jnp.int32, (t_new, FOX_NKV), 1)
    qrows = []
    for h in range(heads):
        kh, gi = divmod(h, g)
        grp = q[:, kh * FOX_NKV:(kh + 1) * FOX_NKV]
        shift = ((kh - gi) * hd) % FOX_NKV
        r = pltpu.roll(grp, shift, axis=1) if shift else grp
        qrows.append(jnp.where((lane >= kh * hd) & (lane < (kh + 1) * hd), r, 0.0))
    qbd = jnp.concatenate(qrows, axis=0).astype(BF16)
    dcol3 = dcol_ref[0].reshape(heads, t_new, 1)

    def body(ci, carry):
        m, l, acc, run = carry
        _, slot = _paged_chunk_step(pt_ref, ci, nchunks, ppc, pairs, reverse=True)
        s_pages = [None] * ppc
        for i in reversed(range(ppc)):
            lf = lbuf[slot, i]
            y = _suffix_sum_lanes(lf)
            bias = y - lf + run
            run = run + y[:, 0:1]
            s = _dot(qbd, kbuf[slot, i].astype(BF16))
            s_pages[i] = s.reshape(heads, t_new, PAGE_SIZE) + dcol3 + bias[:, None, :]
        s3 = jnp.concatenate(s_pages, axis=2)
        m_new = jnp.maximum(m, jnp.max(s3, axis=-1, keepdims=True))
        alpha = jnp.exp(m - m_new)
        p = jnp.exp(s3 - m_new)
        l_new = alpha * l + jnp.sum(p, axis=-1, keepdims=True)
        p2 = p.reshape(rows, nk).astype(BF16)
        acc = alpha.reshape(rows, 1) * acc
        for i in range(ppc):
            acc = acc + _dot_nt(p2[:, i * PAGE_SIZE:(i + 1) * PAGE_SIZE], vbuf[slot, i].astype(BF16))
        return m_new, l_new, acc, run

    init = (jnp.full((heads, t_new, 1), MASKED, F32), jnp.zeros((heads, t_new, 1), F32),
            jnp.zeros((rows, FOX_NKV), F32), jnp.zeros((heads, 1), F32))
    m, l, acc, _ = lax.fori_loop(0, nchunks, body, init)

    kn = _pad_rows(kn_ref[...], LANES).astype(BF16)
    vn = _pad_rows(vn_ref[...], LANES).astype(BF16)
    s3 = _dot_nt(qbd, kn).reshape(heads, t_new, LANES) + dcol3 - dtp_ref[0][:, None, :]
    trow = lax.broadcasted_iota(jnp.int32, (t_new, LANES), 0)
    tcol = lax.broadcasted_iota(jnp.int32, (t_new, LANES), 1)
    s3 = jnp.where((tcol <= trow)[None], s3, MASKED)
    m, l, acc = _online_softmax_step(s3, vn, m, l, acc)
    o = acc / l.reshape(rows, 1)

    groups = []
    for kh in range(FOX_KV_HEADS):
        tot = jnp.zeros((t_new, FOX_NKV), F32)
        for gi in range(g):
            h = kh * g + gi
            blk = o[h * t_new:(h + 1) * t_new]
            shift = ((gi - kh) * hd) % FOX_NKV
            r = pltpu.roll(blk, shift, axis=1) if shift else blk
            tot = tot + jnp.where((lane >= gi * hd) & (lane < (gi + 1) * hd), r, 0.0)
        groups.append(tot)
    o_ref[...] = jnp.concatenate(groups, axis=1)


def _fox_sample_attention(q, k_new, v_new, ct, page_table, kt_pages, vt_pages, lf_pages, *, t_new):
    ns = q.shape[0]
    bd, n_pages = page_table.shape
    assert bd * t_new == ns and t_new == SUBLANES
    ppc = _tile(n_pages, 32)
    nchunks = n_pages // ppc
    rows = FOX_HEADS * t_new
    d_bht = ct.reshape(FOX_HEADS, bd, t_new).transpose(1, 0, 2)
    dcol = d_bht.reshape(bd, rows, 1)
    dtp = jnp.pad(d_bht, ((0, 0), (0, 0), (0, LANES - t_new)))
    anyspec = pl.BlockSpec(memory_space=pl.ANY)
    return pl.pallas_call(
        functools.partial(_fox_sample_kernel, ppc=ppc, nchunks=nchunks, t_new=t_new),
        grid_spec=pltpu.PrefetchScalarGridSpec(
            num_scalar_prefetch=1,
            grid=(bd,),
            in_specs=[
                pl.BlockSpec((t_new, FOX_NQ), lambda b, pt: (b, 0)),
                pl.BlockSpec((t_new, FOX_NKV), lambda b, pt: (b, 0)),
                pl.BlockSpec((t_new, FOX_NKV), lambda b, pt: (b, 0)),
                pl.BlockSpec((1, rows, 1), lambda b, pt: (b, 0, 0)),
                pl.BlockSpec((1, FOX_HEADS, LANES), lambda b, pt: (b, 0, 0)),
                anyspec, anyspec, anyspec,
            ],
            out_specs=pl.BlockSpec((t_new, FOX_NQ), lambda b, pt: (b, 0)),
            scratch_shapes=[pltpu.VMEM((2, ppc, FOX_NKV, PAGE_SIZE), F32),
                            pltpu.VMEM((2, ppc, FOX_NKV, PAGE_SIZE), F32),
                            pltpu.VMEM((2, ppc, FOX_HEADS, PAGE_SIZE), F32),
                            pltpu.SemaphoreType.DMA((2,)),
                            pltpu.SemaphoreType.DMA((2,)),
                            pltpu.SemaphoreType.DMA((2,))],
        ),
        out_shape=jax.ShapeDtypeStruct((ns, FOX_NQ), F32),
        compiler_params=_params(("arbitrary",)),
        name="fox_sample_attn",
    )(page_table, q, k_new, v_new, dcol, dtp, kt_pages, vt_pages, lf_pages)


def _oproj_kernel(o_ref, wo_ref, x_ref, g_ref, out_ref):
    y = _dot(o_ref[...].astype(BF16), wo_ref[...])
    out_ref[...] = x_ref[...] + _rms(y, g_ref[...])


def _out_project(o, w_o, x, g):
    n, d = x.shape
    e = o.shape[1]
    tm = _tile(n, 512)
    row = lambda i: (i, 0)
    const = lambda i: (0, 0)
    return pl.pallas_call(
        _oproj_kernel,
        grid=(n // tm,),
        in_specs=[pl.BlockSpec((tm, e), row), pl.BlockSpec((e, d), const),
                  pl.BlockSpec((tm, d), row), pl.BlockSpec((1, d), const)],
        out_specs=pl.BlockSpec((tm, d), row),
        out_shape=jax.ShapeDtypeStruct((n, d), F32),
        compiler_params=_params(("parallel",)),
        name="out_proj",
    )(o, w_o.astype(BF16), x, g.reshape(1, d))


def _mla_oproj_kernel(ol_ref, wuv_ref, wo_ref, x_ref, g_ref, out_ref):
    parts = []
    for j in range(MLA_HEADS // 2):
        pair = jnp.concatenate([ol_ref[2 * j], ol_ref[2 * j + 1]], axis=1).astype(BF16)
        parts.append(_dot(pair, wuv_ref[j]).astype(BF16))
    y = _dot(jnp.concatenate(parts, axis=1), wo_ref[...])
    out_ref[...] = x_ref[...] + _rms(y, g_ref[...])


def _mla_out_project(o_lat, w_uv, w_o, x, g):
    n, d = x.shape
    tm = _tile(n, 512)
    npair = MLA_HEADS // 2
    wp = w_uv.reshape(MLA_KV_LORA, npair, 2, MLA_V_DIM).transpose(1, 2, 0, 3)
    z = jnp.zeros_like(wp[:, 0])
    wbd = jnp.concatenate([jnp.concatenate([wp[:, 0], z], axis=2),
                           jnp.concatenate([z, wp[:, 1]], axis=2)], axis=1).astype(BF16)
    row = lambda i: (i, 0)
    const = lambda i: (0, 0)
    return pl.pallas_call(
        _mla_oproj_kernel,
        grid=(n // tm,),
        in_specs=[pl.BlockSpec((MLA_HEADS, tm, MLA_KV_LORA), lambda i: (0, i, 0)),
                  pl.BlockSpec(wbd.shape, lambda i: (0, 0, 0)),
                  pl.BlockSpec(w_o.shape, const),
                  pl.BlockSpec((tm, d), row), pl.BlockSpec((1, d), const)],
        out_specs=pl.BlockSpec((tm, d), row),
        out_shape=jax.ShapeDtypeStruct((n, d), F32),
        compiler_params=_params(("parallel",)),
        name="mla_out_proj",
    )(o_lat, wbd, w_o.astype(BF16), x, g.reshape(1, d))


def _mla_proj_kernel(x_ref, g_ref, wa_ref, qn_ref, kvn_ref, wnope_ref, wpe_ref, wper_ref, wuk_ref,
                     cos_ref, sin_ref, qp_ref, kvp_ref, ckv_ref, ckvt_ref, kpe_ref, *, logit_scale):
    tm = x_ref.shape[0]
    scale = (MLA_NOPE + MLA_ROPE) ** -0.5 * logit_scale
    h = _rms(x_ref[...], g_ref[...]).astype(BF16)
    a = _dot(h, wa_ref[...])
    cos, sin = cos_ref[...], sin_ref[...]
    c_q = _rms(a[:, :MLA_Q_LORA], qn_ref[...]).astype(BF16)
    c_kv = _rms(a[:, MLA_Q_LORA:MLA_Q_LORA + MLA_KV_LORA], kvn_ref[...])
    r0 = MLA_Q_LORA + MLA_KV_LORA
    kpe_rep = a[:, r0:r0 + LANES] * cos + a[:, r0 + LANES:r0 + 2 * LANES] * sin
    ckv_ref[...] = c_kv
    ckvt_ref[0:MLA_KV_LORA, :] = c_kv.T.astype(BF16)
    ckvt_ref[MLA_KV_LORA:MLA_KV_LORA + ONES_ROWS, :] = jnp.ones((ONES_ROWS, tm), BF16)
    kpe_ref[...] = kpe_rep[:, :MLA_ROPE]
    kvp_ref[...] = jnp.concatenate([c_kv, kpe_rep], axis=1).astype(BF16)

    nrep = MLA_HEADS // ROPE_REP
    cos_all = jnp.concatenate([cos] * nrep, axis=1)
    sin_all = jnp.concatenate([sin] * nrep, axis=1)
    qpe = (_dot(c_q, wpe_ref[...]) * cos_all + _dot(c_q, wper_ref[...]) * sin_all) * scale
    qnope = _dot(c_q, wnope_ref[...]).astype(BF16)
    lane = lax.broadcasted_iota(jnp.int32, (tm, LANES), 1)
    for j in range(MLA_HEADS // 2):
        qlat2 = _dot(qnope[:, j * LANES:(j + 1) * LANES], wuk_ref[j]) * scale
        for w in range(2):
            hh = 2 * j + w
            grp = qpe[:, (hh // ROPE_REP) * LANES:(hh // ROPE_REP + 1) * LANES]
            quarter = hh % ROPE_REP
            pe = jnp.where((lane >= quarter * MLA_ROPE) & (lane < (quarter + 1) * MLA_ROPE), grp, 0.0)
            qp_ref[hh] = jnp.concatenate([qlat2[:, w * LANES:(w + 1) * LANES], pe], axis=1).astype(qp_ref.dtype)


def _rot_half_cols(w):
    half = MLA_ROPE // 2
    return jnp.concatenate([-w[..., half:], w[..., :half]], axis=-1)


def _mla_project(x, g, w_a, q_norm, kv_norm, w_uq, w_uk, pos, *, q_dtype, logit_scale):
    n, d = x.shape
    tm = _tile(n, 256)
    npos = pos.shape[0]
    assert n % npos == 0 and npos % tm == 0
    grid = (n // npos, npos // tm)
    g1 = grid[1]

    half = MLA_ROPE // 2
    inv = ROPE_THETA ** (-jnp.arange(half, dtype=F32) / half)
    ang = pos.astype(F32)[:, None] * inv[None, :]
    cos = jnp.tile(jnp.cos(ang), (1, 2 * ROPE_REP))
    sin = jnp.tile(jnp.sin(ang), (1, 2 * ROPE_REP))

    r0 = MLA_Q_LORA + MLA_KV_LORA
    w_pe = w_a[:, r0:]
    wa_ext = jnp.concatenate([w_a[:, :r0], jnp.tile(w_pe, (1, ROPE_REP)),
                              jnp.tile(_rot_half_cols(w_pe), (1, ROPE_REP))], axis=1).astype(BF16)
    w_nope = w_uq[:, :, :MLA_NOPE].reshape(MLA_Q_LORA, MLA_HEADS * MLA_NOPE).astype(BF16)
    w_qpe = w_uq[:, :, MLA_NOPE:]
    w_pe_q = w_qpe.reshape(MLA_Q_LORA, MLA_HEADS * MLA_ROPE).astype(BF16)
    w_pe_qr = _rot_half_cols(w_qpe).reshape(MLA_Q_LORA, MLA_HEADS * MLA_ROPE).astype(BF16)
    npair = MLA_HEADS // 2
    wk = w_uk.reshape(MLA_KV_LORA, npair, 2, MLA_NOPE).transpose(1, 2, 3, 0)
    z = jnp.zeros_like(wk[:, 0])
    wuk_bd = jnp.concatenate([jnp.concatenate([wk[:, 0], z], axis=2),
                              jnp.concatenate([z, wk[:, 1]], axis=2)], axis=1).astype(BF16)

    tok = lambda i, j: (i * g1 + j, 0)
    const = lambda i, j: (0, 0)
    posb = lambda i, j: (j, 0)
    return pl.pallas_call(
        functools.partial(_mla_proj_kernel, logit_scale=logit_scale),
        grid=grid,
        in_specs=[
            pl.BlockSpec((tm, d), tok),
            pl.BlockSpec((1, d), const),
            pl.BlockSpec(wa_ext.shape, const),
            pl.BlockSpec((1, MLA_Q_LORA), const),
            pl.BlockSpec((1, MLA_KV_LORA), const),
            pl.BlockSpec(w_nope.shape, const),
            pl.BlockSpec(w_pe_q.shape, const),
            pl.BlockSpec(w_pe_qr.shape, const),
            pl.BlockSpec(wuk_bd.shape, lambda i, j: (0, 0, 0)),
            pl.BlockSpec((tm, LANES), posb),
            pl.BlockSpec((tm, LANES), posb),
        ],
        out_specs=[
            pl.BlockSpec((MLA_HEADS, tm, MLA_QK), lambda i, j: (0, i * g1 + j, 0)),
            pl.BlockSpec((tm, MLA_QK), tok),
            pl.BlockSpec((tm, MLA_KV_LORA), tok),
            pl.BlockSpec((MLA_KV_LORA + ONES_ROWS, tm), lambda i, j: (0, i * g1 + j)),
            pl.BlockSpec((tm, MLA_ROPE), tok),
        ],
        out_shape=[
            jax.ShapeDtypeStruct((MLA_HEADS, n, MLA_QK), q_dtype),
            jax.ShapeDtypeStruct((n, MLA_QK), BF16),
            jax.ShapeDtypeStruct((n, MLA_KV_LORA), F32),
            jax.ShapeDtypeStruct((MLA_KV_LORA + ONES_ROWS, n), BF16),
            jax.ShapeDtypeStruct((n, MLA_ROPE), F32),
        ],
        compiler_params=_params(("parallel", "parallel")),
        name="mla_proj",
    )(x, g.reshape(1, d), wa_ext, q_norm.reshape(1, -1), kv_norm.reshape(1, -1),
      w_nope, w_pe_q, w_pe_qr, wuk_bd, cos, sin)


def _mla_attn_kernel(q_ref, kv_ref, ckvt_ref, o_ref, *, tq, tk):
    qi = pl.program_id(1)
    nh = MLA_HEADS
    ngroups = 4
    cols = nh * tq // ngroups
    q = q_ref[...].reshape(nh * tq, MLA_QK)
    qs = [q[i * cols:(i + 1) * cols] for i in range(ngroups)]
    nfull = (qi * tq) // tk

    def logits(j):
        kv = kv_ref[0, pl.ds(pl.multiple_of(j * tk, tk), tk), :]
        return [_dot_nt(kv, qx) for qx in qs]

    def consume(j, sts, states, masked):
        off = pl.multiple_of(j * tk, tk)
        if masked:
            sts = [_causal_mask_t(st, off, qi * tq, tq) for st in sts]
        return _softmax_steps_t(sts, ckvt_ref[:, pl.ds(off, tk)], states, [None] * ngroups)

    init = tuple((jnp.full((1, cols), MASKED, F32), jnp.zeros((MLA_KV_LORA + ONES_ROWS, cols), F32))
                 for _ in range(ngroups))
    states = lax.fori_loop(0, nfull, lambda j, s: consume(j, logits(j), s, False), init)
    ot = jnp.concatenate([_normalised_t(acc, MLA_KV_LORA)
                          for _, acc in consume(nfull, logits(nfull), states, True)], axis=1)
    o_ref[...] = ot.T.reshape(nh, tq, MLA_KV_LORA).astype(o_ref.dtype)


def _mla_prompt_attention(qp, kvp, ckvt, *, batch, seq_len):
    n = batch * seq_len
    tq = _tile(seq_len, 64)
    tk = _tile(seq_len, 512)
    assert tk % tq == 0
    nq = seq_len // tq
    kv3 = kvp.reshape(batch, seq_len, MLA_QK)
    return pl.pallas_call(
        functools.partial(_mla_attn_kernel, tq=tq, tk=tk),
        grid=(batch, nq),
        in_specs=[pl.BlockSpec((MLA_HEADS, tq, MLA_QK), lambda b, i: (0, b * nq + i, 0)),
                  pl.BlockSpec((1, seq_len, MLA_QK), lambda b, i: (b, 0, 0)),
                  pl.BlockSpec((MLA_KV_LORA + ONES_ROWS, seq_len), lambda b, i: (0, b))],
        out_specs=pl.BlockSpec((MLA_HEADS, tq, MLA_KV_LORA), lambda b, i: (0, b * nq + i, 0)),
        out_shape=jax.ShapeDtypeStruct((MLA_HEADS, n, MLA_KV_LORA), BF16),
        compiler_params=_params(("parallel", "arbitrary")),
        name="mla_prompt_attn",
    )(qp, kv3, ckvt)


def _mla_sample_kernel(pt_ref, q_ref, cn_ref, rn_ref, c_hbm, r_hbm, o_ref, cbuf, rbuf, csem, rsem,
                       *, ppc, nchunks, t_new):
    nk = ppc * PAGE_SIZE
    nh = MLA_HEADS
    rows = nh * t_new
    pairs = ((c_hbm, cbuf, csem), (r_hbm, rbuf, rsem))

    q = q_ref[...].reshape(rows, MLA_QK)
    qlat = q[:, :MLA_KV_LORA].astype(BF16)
    grp = q[:, MLA_KV_LORA:]
    qpe = grp[:, :MLA_ROPE]
    for i in range(1, ROPE_REP):
        qpe = qpe + grp[:, i * MLA_ROPE:(i + 1) * MLA_ROPE]
    qpe = qpe.astype(BF16)

    def body(c, carry):
        _, slot = _paged_chunk_step(pt_ref, c, nchunks, ppc, pairs, reverse=False)
        ckv = cbuf[slot].reshape(nk, MLA_KV_LORA).astype(BF16)
        s_pe = jnp.concatenate([_dot(qpe, rbuf[slot, i].astype(BF16)) for i in range(ppc)], axis=1)
        s3 = (_dot_nt(qlat, ckv) + s_pe).reshape(nh, t_new, nk)
        return _online_softmax_step(s3, ckv, *carry)

    init = (jnp.full((nh, t_new, 1), MASKED, F32), jnp.zeros((nh, t_new, 1), F32),
            jnp.zeros((rows, MLA_KV_LORA), F32))
    m, l, acc = lax.fori_loop(0, nchunks, body, init)

    cn = _pad_rows(cn_ref[...], LANES).astype(BF16)
    rn = _pad_rows(rn_ref[...], LANES).astype(BF16)
    s3 = (_dot_nt(qlat, cn) + _dot_nt(qpe, rn)).reshape(nh, t_new, LANES)
    trow = lax.broadcasted_iota(jnp.int32, (t_new, LANES), 0)
    tcol = lax.broadcasted_iota(jnp.int32, (t_new, LANES), 1)
    s3 = jnp.where((tcol <= trow)[None], s3, MASKED)
    m, l, acc = _online_softmax_step(s3, cn, m, l, acc)
    o_ref[...] = (acc / l.reshape(rows, 1)).reshape(nh, t_new, MLA_KV_LORA)


def _mla_sample_attention(qp, c_new, kpe_new, page_table, ckv_pages, kpet_pages, *, t_new):
    ns = c_new.shape[0]
    bd, n_pages = page_table.shape
    assert bd * t_new == ns and t_new == SUBLANES
    ppc = _tile(n_pages, 32)
    nchunks = n_pages // ppc
    return pl.pallas_call(
        functools.partial(_mla_sample_kernel, ppc=ppc, nchunks=nchunks, t_new=t_new),
        grid_spec=pltpu.PrefetchScalarGridSpec(
            num_scalar_prefetch=1,
            grid=(bd,),
            in_specs=[
                pl.BlockSpec((MLA_HEADS, t_new, MLA_QK), lambda b, pt: (0, b, 0)),
                pl.BlockSpec((t_new, MLA_KV_LORA), lambda b, pt: (b, 0)),
                pl.BlockSpec((t_new, MLA_ROPE), lambda b, pt: (b, 0)),
                pl.BlockSpec(memory_space=pl.ANY),
                pl.BlockSpec(memory_space=pl.ANY),
            ],
            out_specs=pl.BlockSpec((MLA_HEADS, t_new, MLA_KV_LORA), lambda b, pt: (0, b, 0)),
            scratch_shapes=[pltpu.VMEM((2, ppc, PAGE_SIZE, MLA_KV_LORA), F32),
                            pltpu.VMEM((2, ppc, MLA_ROPE, PAGE_SIZE), F32),
                            pltpu.SemaphoreType.DMA((2,)),
                            pltpu.SemaphoreType.DMA((2,))],
        ),
        out_shape=jax.ShapeDtypeStruct((MLA_HEADS, ns, MLA_KV_LORA), F32),
        compiler_params=_params(("arbitrary",)),
        name="mla_sample_attn",
    )(page_table, qp, c_new, kpe_new, ckv_pages, kpet_pages)


def _ffn_kernel(x_ref, g1_ref, g2_ref, win_ref, cw_ref, cb_ref, wout_ref, *rest, fc, seq_rows, chained):
    if chained:
        out_ref, st_out_ref, act_ref, carry_ref = rest
    else:
        st_ref, out_ref, st_out_ref, act_ref = rest
    tm, d = x_ref.shape
    dff = wout_ref.shape[0]
    x = x_ref[...]
    h = _rms(x, g1_ref[...]).astype(BF16)

    if chained:
        @pl.when(pl.program_id(1) == 0)
        def _():
            carry_ref[...] = jnp.zeros_like(carry_ref)
        row = lax.broadcasted_iota(jnp.int32, (tm, fc), 0)
    else:
        nseq = tm // seq_rows
        row = lax.broadcasted_iota(jnp.int32, (nseq, seq_rows, fc), 1)

    for f in range(dff // fc):
        cs = slice(f * fc, (f + 1) * fc)
        gate = _dot(h, win_ref[:, cs])
        up = _dot(h, win_ref[:, dff + f * fc:dff + (f + 1) * fc])
        w0, w1, w2 = cw_ref[0:1, cs], cw_ref[1:2, cs], cw_ref[2:3, cs]
        if chained:
            p0, p1 = carry_ref[0:1, cs], carry_ref[1:2, cs]
            sh1 = jnp.where(row == 0, p1, pltpu.roll(gate, 1, axis=0))
            sh2 = jnp.where(row == 0, p0, jnp.where(row == 1, p1, pltpu.roll(gate, 2, axis=0)))
            carry_ref[0:2, cs] = gate[tm - 2:tm]
            st_out_ref[0, :, cs] = gate[tm - 2:tm]
            conv = cb_ref[:, cs] + w0 * sh2 + w1 * sh1 + w2 * gate
        else:
            g3 = gate.reshape(nseq, seq_rows, fc)
            p0, p1 = st_ref[:, 0:1, cs], st_ref[:, 1:2, cs]
            sh1 = jnp.where(row == 0, p1, pltpu.roll(g3, 1, axis=1))
            sh2 = jnp.where(row == 0, p0, jnp.where(row == 1, p1, pltpu.roll(g3, 2, axis=1)))
            st_out_ref[:, :, cs] = g3[:, seq_rows - 2:seq_rows, :]
            conv = (cb_ref[:, cs] + w0 * sh2 + w1 * sh1 + w2 * g3).reshape(tm, fc)
        act_ref[:, cs] = (jax.nn.gelu(conv, approximate=True) * up).astype(BF16)

    y = _dot(act_ref[...], wout_ref[...])
    out_ref[...] = x + _rms(y, g2_ref[...])


def _conv_ffn_block(x, g1, g2, w_in, conv_w, conv_b, w_out, state, *, seq_len):
    n, d = x.shape
    dff = w_out.shape[0]
    fc = 256
    assert dff % fc == 0 and CONV_W == 3
    nseq_total = n // seq_len
    chained = state is None
    wspec = dict(pipeline_mode=pl.Buffered(1))
    if chained:
        tm = _tile(seq_len, 512)
        grid = (nseq_total, seq_len // tm)
        g1n = grid[1]
        tok = lambda i, j: (i * g1n + j, 0)
        const = lambda i, j: (0, 0)
        st_spec = pl.BlockSpec((1, CONV_W - 1, dff), lambda i, j: (i, 0, 0))
        extra_in, extra_specs = [], []
        scratch = [pltpu.VMEM((tm, dff), BF16), pltpu.VMEM((SUBLANES, dff), F32)]
        sem = ("arbitrary", "arbitrary")
    else:
        sb = _tile(nseq_total, 32)
        tm = sb * seq_len
        grid = (nseq_total // sb,)
        tok = lambda i: (i, 0)
        const = lambda i: (0, 0)
        st_spec = pl.BlockSpec((sb, CONV_W - 1, dff), lambda i: (i, 0, 0))
        extra_in, extra_specs = [state], [st_spec]
        scratch = [pltpu.VMEM((tm, dff), BF16)]
        sem = ("arbitrary",)
    out, st_new = pl.pallas_call(
        functools.partial(_ffn_kernel, fc=fc, seq_rows=seq_len, chained=chained),
        grid=grid,
        in_specs=[
            pl.BlockSpec((tm, d), tok),
            pl.BlockSpec((1, d), const),
            pl.BlockSpec((1, d), const),
            pl.BlockSpec((d, 2 * dff), const, **wspec),
            pl.BlockSpec((CONV_W, dff), const),
            pl.BlockSpec((1, dff), const),
            pl.BlockSpec((dff, d), const, **wspec),
        ] + extra_specs,
        out_specs=[pl.BlockSpec((tm, d), tok), st_spec],
        out_shape=[jax.ShapeDtypeStruct((n, d), F32),
                   jax.ShapeDtypeStruct((nseq_total, CONV_W - 1, dff), F32)],
        scratch_shapes=scratch,
        compiler_params=_params(sem),
        name="conv_ffn",
    )(x, g1.reshape(1, d), g2.reshape(1, d), w_in.astype(BF16), conv_w, conv_b.reshape(1, dff),
      w_out.astype(BF16), *extra_in)
    return out, st_new


def kernel(x_prompt, x_sample, cache_fox_k, cache_fox_v, cache_fox_logf, cache_mla_ckv, cache_mla_kpe, state_conv, page_table, norm_mix_pre, norm_mix_post, norm_ffn_pre, norm_ffn_post, fox_w_in, fox_b_f, fox_w_o, mla_w_a, mla_q_norm, mla_kv_norm, mla_w_uq, mla_w_uk, mla_w_uv, mla_w_o, ffn_w_in, ffn_conv_w, ffn_conv_b, ffn_w_out):
    batch, seq_len, d = x_prompt.shape
    bd, t_new, _ = x_sample.shape
    depth = norm_mix_pre.shape[0]
    n_pool = cache_fox_k.shape[1]
    past = page_table.shape[1] * PAGE_SIZE
    xp = x_prompt.reshape(batch * seq_len, d)
    xs = x_sample.reshape(bd * t_new, d)

    fox_kt_pages = jnp.transpose(cache_fox_k, (0, 1, 3, 4, 2)).reshape(-1, FOX_NKV, PAGE_SIZE)
    fox_vt_pages = jnp.transpose(cache_fox_v, (0, 1, 3, 4, 2)).reshape(-1, FOX_NKV, PAGE_SIZE)
    fox_f_pages = jnp.transpose(cache_fox_logf, (0, 1, 3, 2)).reshape(-1, FOX_HEADS, PAGE_SIZE)
    mla_c_pages = cache_mla_ckv.reshape(-1, PAGE_SIZE, MLA_KV_LORA)
    mla_rt_pages = jnp.transpose(cache_mla_kpe, (0, 1, 3, 2)).reshape(-1, MLA_ROPE, PAGE_SIZE)

    outs_p = {k: [] for k in ("fk", "fv", "ff", "mc", "mr", "cv")}
    outs_s = {k: [] for k in ("fk", "fv", "ff", "mc", "mr", "cv")}
    for i in range(depth):
        li = i // N_MIXERS
        pages = page_table + li * n_pool
        if i % N_MIXERS == 0:
            w_in, b_f, w_o = fox_w_in[li], fox_b_f[li], fox_w_o[li]
            q, k, v, ka, kb, vt, logf, ct = _fox_project(xp, norm_mix_pre[i], w_in, b_f, seq_len=seq_len,
                                                         q_dtype=BF16, logit_scale=LOG2E)
            o = _fox_prompt_attention(q, ka, kb, vt, ct, batch=batch, seq_len=seq_len)
            xp = _out_project(o, w_o, xp, norm_mix_post[i])
            outs_p["fk"].append(k.reshape(batch, seq_len, FOX_KV_HEADS, FOX_HEAD_DIM))
            outs_p["fv"].append(v.reshape(batch, seq_len, FOX_KV_HEADS, FOX_HEAD_DIM))
            outs_p["ff"].append(logf.reshape(batch, seq_len, FOX_HEADS))

            q, k, v, _, _, _, logf, ct = _fox_project(xs, norm_mix_pre[i], w_in, b_f, seq_len=t_new,
                                                      q_dtype=F32, logit_scale=1.0)
            o = _fox_sample_attention(q, k, v, ct, pages, fox_kt_pages, fox_vt_pages, fox_f_pages, t_new=t_new)
            xs = _out_project(o, w_o, xs, norm_mix_post[i])
            outs_s["fk"].append(k.reshape(bd, t_new, FOX_KV_HEADS, FOX_HEAD_DIM))
            outs_s["fv"].append(v.reshape(bd, t_new, FOX_KV_HEADS, FOX_HEAD_DIM))
            outs_s["ff"].append(logf.reshape(bd, t_new, FOX_HEADS))
        else:
            args = (mla_w_a[li], mla_q_norm[li], mla_kv_norm[li], mla_w_uq[li], mla_w_uk[li])
            qp, kvp, ckv, ckvt, kpe = _mla_project(xp, norm_mix_pre[i], *args, jnp.arange(seq_len),
                                                   q_dtype=BF16, logit_scale=LOG2E)
            o_lat = _mla_prompt_attention(qp, kvp, ckvt, batch=batch, seq_len=seq_len)
            xp = _mla_out_project(o_lat, mla_w_uv[li], mla_w_o[li], xp, norm_mix_post[i])
            outs_p["mc"].append(ckv.reshape(batch, seq_len, MLA_KV_LORA))
            outs_p["mr"].append(kpe.reshape(batch, seq_len, MLA_ROPE))

            pos_s = jnp.tile(past + jnp.arange(t_new), bd)
            qp, _, ckv, _, kpe = _mla_project(xs, norm_mix_pre[i], *args, pos_s, q_dtype=F32, logit_scale=1.0)
            o_lat = _mla_sample_attention(qp, ckv, kpe, pages, mla_c_pages, mla_rt_pages, t_new=t_new)
            xs = _mla_out_project(o_lat, mla_w_uv[li], mla_w_o[li], xs, norm_mix_post[i])
            outs_s["mc"].append(ckv.reshape(bd, t_new, MLA_KV_LORA))
            outs_s["mr"].append(kpe.reshape(bd, t_new, MLA_ROPE))

        ffn = (norm_ffn_pre[i], norm_ffn_post[i], ffn_w_in[i], ffn_conv_w[i], ffn_conv_b[i], ffn_w_out[i])
        xp, cvp = _conv_ffn_block(xp, *ffn, None, seq_len=seq_len)
        xs, cvs = _conv_ffn_block(xs, *ffn, state_conv[i], seq_len=t_new)
        outs_p["cv"].append(cvp)
        outs_s["cv"].append(cvs)

    ldt = cache_fox_logf.dtype
    st = jnp.stack
    return (xp.reshape(batch, seq_len, d), xs.reshape(bd, t_new, d),
            st(outs_p["fk"]), st(outs_p["fv"]), st(outs_p["ff"]).astype(ldt), st(outs_p["mc"]), st(outs_p["mr"]), st(outs_p["cv"]),
            st(outs_s["fk"]), st(outs_s["fv"]), st(outs_s["ff"]).astype(ldt), st(outs_s["mc"]), st(outs_s["mr"]), st(outs_s["cv"]))
```

```python
import functools

import jax
import jax.numpy as jnp
from jax import lax
from jax.experimental import pallas as pl
from jax.experimental.pallas import tpu as pltpu

F32 = jnp.float32
BF16 = jnp.bfloat16

N_MIXERS = 2
FOX_HEADS = 16
FOX_KV_HEADS = 4
FOX_GROUP = FOX_HEADS // FOX_KV_HEADS
FOX_HEAD_DIM = 64
FOX_NQ = FOX_HEADS * FOX_HEAD_DIM
FOX_NKV = FOX_KV_HEADS * FOX_HEAD_DIM
MLA_HEADS = 16
MLA_Q_LORA = 256
MLA_KV_LORA = 128
MLA_NOPE = 64
MLA_ROPE = 32
MLA_V_DIM = 64
ROPE_THETA = 10000.0
CONV_W = 3
PAGE_SIZE = 128
EPS = 1e-6

LANES = 128
SUBLANES = 8
V7X_VMEM_LIMIT_BYTES = 56 * 1024 * 1024

MASKED = -1e30
MLA_QK = 2 * LANES
ROPE_REP = LANES // MLA_ROPE

LOG2E = 1.4426950408889634
ONES_ROWS = 16
PAGE_GROUPS = 4
FOX_BIAS_PIECES = 3
FOX_BIAS_LANE0 = FOX_HEAD_DIM

NT_DIMS = (((1,), (1,)), ((), ()))


def _params(sem):
    return pltpu.CompilerParams(dimension_semantics=sem, vmem_limit_bytes=V7X_VMEM_LIMIT_BYTES)


def _tile(n, pref):
    t = min(n, pref)
    assert n % t == 0, (n, pref)
    return t


def _rms(x, g):
    return x * lax.rsqrt(jnp.mean(x * x, axis=-1, keepdims=True) + EPS) * g


def _log_sigmoid(x):
    return jnp.minimum(x, 0.0) - jnp.log1p(jnp.exp(-jnp.abs(x)))


def _split3(x):
    hi = x.astype(BF16)
    r1 = x - hi.astype(F32)
    mid = r1.astype(BF16)
    lo = (r1 - mid.astype(F32)).astype(BF16)
    return hi, mid, lo


def _dot(a, b):
    return jnp.dot(a, b, preferred_element_type=F32)


def _dot_nt(a, b):
    return lax.dot_general(a, b, NT_DIMS, preferred_element_type=F32)


def _fox_proj_kernel(x_ref, g_ref, w_ref, wft_ref, bf_ref, bft_ref, u_ref, l_ref, pa_ref, pb_ref,
                     q_ref, k_ref, v_ref, ktf_ref, vtf_ref, ka_ref, kb_ref, vt_ref, logf_ref, ct_ref,
                     carry_ref, carryt_ref, *, logit_scale):
    tm = x_ref.shape[0]
    hd = FOX_HEAD_DIM

    @pl.when(pl.program_id(1) == 0)
    def _():
        carry_ref[...] = jnp.zeros_like(carry_ref)
        carryt_ref[...] = jnp.zeros_like(carryt_ref)

    h = _rms(x_ref[...], g_ref[...]).astype(BF16)
    proj = _dot(h, w_ref[...])
    q_ref[...] = (proj[:, :FOX_NQ] * (hd ** -0.5 * logit_scale)).astype(q_ref.dtype)
    k = proj[:, FOX_NQ:FOX_NQ + FOX_NKV]
    v = proj[:, FOX_NQ + FOX_NKV:FOX_NQ + 2 * FOX_NKV]
    k_ref[...] = k
    v_ref[...] = v
    ktf_ref[0] = k.T
    vtf_ref[0] = v.T
    vt = vtf_ref[0].astype(BF16)
    for kh in range(FOX_KV_HEADS):
        vt_ref[kh, 0:hd, :] = vt[kh * hd:(kh + 1) * hd]
        vt_ref[kh, hd:hd + ONES_ROWS, :] = jnp.ones((ONES_ROWS, tm), BF16)
    f0 = FOX_NQ + 2 * FOX_NKV
    logf = _log_sigmoid(proj[:, f0:f0 + FOX_HEADS] + bf_ref[...])
    logf_ref[...] = logf
    lmat = l_ref[...]
    c = carry_ref[...]
    for piece in _split3(logf):
        c = c + _dot(lmat, piece)
    carry_ref[...] = c[tm - 1:tm, :]
    placed_a = jnp.zeros((tm, FOX_KV_HEADS * LANES), F32)
    placed_b = jnp.zeros((tm, FOX_KV_HEADS * LANES), F32)
    for p, piece in enumerate(_split3(c * logit_scale)):
        placed_a = placed_a + _dot(piece, pa_ref[p])
        placed_b = placed_b + _dot(piece, pb_ref[p])
    lane = lax.broadcasted_iota(jnp.int32, (tm, LANES), 1)
    for kh in range(FOX_KV_HEADS):
        grp = k[:, (kh // 2) * LANES:(kh // 2 + 1) * LANES]
        swapped = pltpu.roll(grp, hd, axis=1)
        low, high = (grp, swapped) if kh % 2 == 0 else (swapped, grp)
        sl = slice(kh * LANES, (kh + 1) * LANES)
        ka_ref[kh] = jnp.where(lane < hd, low, placed_a[:, sl]).astype(BF16)
        kb_ref[kh] = jnp.where(lane >= hd, high, placed_b[:, sl]).astype(BF16)
    logft = _log_sigmoid(_dot_nt(wft_ref[...], h) + bft_ref[...])
    umat = u_ref[...]
    ct = carryt_ref[...]
    for piece in _split3(logft):
        ct = ct + _dot(piece, umat)
    ct_ref[...] = ct
    carryt_ref[...] = ct[:, tm - 1:tm]


def _fox_project(x, g, w_in, b_f, *, seq_len, q_dtype, logit_scale):
    n, d = x.shape
    tm = _tile(n, 512)
    if seq_len >= tm:
        assert seq_len % tm == 0
        grid = (n // seq_len, seq_len // tm)
        grp = jnp.zeros((tm,), jnp.int32)
    else:
        assert tm % seq_len == 0
        grid = (n // tm, 1)
        grp = jnp.arange(tm, dtype=jnp.int32) // seq_len
    g1 = grid[1]
    pos = jnp.arange(tm, dtype=jnp.int32)
    same = grp[:, None] == grp[None, :]
    umat = (same & (pos[:, None] <= pos[None, :])).astype(BF16)
    lmat = umat.T
    f0 = FOX_NQ + 2 * FOX_NKV
    w_main = jnp.pad(w_in, ((0, 0), (0, LANES - FOX_HEADS))).astype(BF16)
    wft = w_in[:, f0:].T.astype(BF16)
    wcols = w_main.shape[1]
    hh = jnp.arange(FOX_HEADS)
    pp = jnp.arange(FOX_BIAS_PIECES)
    col_b = (hh // FOX_GROUP)[None, :] * LANES + FOX_BIAS_PIECES * (hh % FOX_GROUP)[None, :] + pp[:, None]
    cols = jnp.arange(FOX_KV_HEADS * LANES)
    place_b = (col_b[:, :, None] == cols[None, None, :]).astype(BF16)
    place_a = (col_b[:, :, None] + FOX_HEAD_DIM == cols[None, None, :]).astype(BF16)

    tok = lambda i, j: (i * g1 + j, 0)
    tok3 = lambda i, j: (0, i * g1 + j, 0)
    tokt = lambda i, j: (0, i * g1 + j)
    const = lambda i, j: (0, 0)
    const3 = lambda i, j: (0, 0, 0)
    outs = pl.pallas_call(
        functools.partial(_fox_proj_kernel, logit_scale=logit_scale),
        grid=grid,
        in_specs=[
            pl.BlockSpec((tm, d), tok),
            pl.BlockSpec((1, d), const),
            pl.BlockSpec((d, wcols), const),
            pl.BlockSpec((FOX_HEADS, d), const),
            pl.BlockSpec((1, FOX_HEADS), const),
            pl.BlockSpec((FOX_HEADS, 1), const),
            pl.BlockSpec((tm, tm), const),
            pl.BlockSpec((tm, tm), const),
            pl.BlockSpec(place_a.shape, const3),
            pl.BlockSpec(place_b.shape, const3),
        ],
        out_specs=[
            pl.BlockSpec((tm, FOX_NQ), tok),
            pl.BlockSpec((tm, FOX_NKV), tok),
            pl.BlockSpec((tm, FOX_NKV), tok),
            pl.BlockSpec((1, FOX_NKV, tm), lambda i, j: (i, 0, j)),
            pl.BlockSpec((1, FOX_NKV, tm), lambda i, j: (i, 0, j)),
            pl.BlockSpec((FOX_KV_HEADS, tm, LANES), tok3),
            pl.BlockSpec((FOX_KV_HEADS, tm, LANES), tok3),
            pl.BlockSpec((FOX_KV_HEADS, FOX_HEAD_DIM + ONES_ROWS, tm), lambda i, j: (0, 0, i * g1 + j)),
            pl.BlockSpec((tm, FOX_HEADS), tok),
            pl.BlockSpec((FOX_HEADS, tm), tokt),
        ],
        out_shape=[
            jax.ShapeDtypeStruct((n, FOX_NQ), q_dtype),
            jax.ShapeDtypeStruct((n, FOX_NKV), F32),
            jax.ShapeDtypeStruct((n, FOX_NKV), F32),
            jax.ShapeDtypeStruct((grid[0], FOX_NKV, g1 * tm), F32),
            jax.ShapeDtypeStruct((grid[0], FOX_NKV, g1 * tm), F32),
            jax.ShapeDtypeStruct((FOX_KV_HEADS, n, LANES), BF16),
            jax.ShapeDtypeStruct((FOX_KV_HEADS, n, LANES), BF16),
            jax.ShapeDtypeStruct((FOX_KV_HEADS, FOX_HEAD_DIM + ONES_ROWS, n), BF16),
            jax.ShapeDtypeStruct((n, FOX_HEADS), F32),
            jax.ShapeDtypeStruct((FOX_HEADS, n), F32),
        ],
        scratch_shapes=[pltpu.VMEM((1, FOX_HEADS), F32), pltpu.VMEM((FOX_HEADS, 1), F32)],
        compiler_params=_params(("arbitrary", "arbitrary")),
        name="fox_proj",
    )(x, g.reshape(1, d), w_main, wft, b_f.reshape(1, FOX_HEADS), b_f.reshape(FOX_HEADS, 1), umat, lmat,
      place_a, place_b)
    return outs


def _online_softmax_step(s, v, m, l, acc):
    rows, n = acc.shape[0], s.shape[-1]
    m_new = jnp.maximum(m, jnp.max(s, axis=-1, keepdims=True))
    alpha = jnp.exp(m - m_new)
    p = jnp.exp(s - m_new)
    l_new = alpha * l + jnp.sum(p, axis=-1, keepdims=True)
    pv = _dot(p.reshape(rows, n).astype(BF16), v)
    return m_new, l_new, alpha.reshape(rows, 1) * acc + pv


def _local_softmax(s):
    m = jnp.max(s, axis=-1, keepdims=True)
    p = jnp.exp(s - m)
    return m, jnp.sum(p, axis=-1, keepdims=True), p


def _merge_softmax(m, l, acc, parts):
    rows = acc.shape[0]
    m_new = m
    for mu, _, _ in parts:
        m_new = jnp.maximum(m_new, mu)
    w = jnp.exp(m - m_new)
    l_new, acc_new = w * l, w.reshape(rows, 1) * acc
    for mu, lu, ou in parts:
        w = jnp.exp(mu - m_new)
        l_new, acc_new = l_new + w * lu, acc_new + w.reshape(rows, 1) * ou
    return m_new, l_new, acc_new


def _flash_tiles_t(tiles, logit_fn, mask_fn, vt1_fn, cqs, states):
    ngroups = len(states)
    sts = [[logit_fn(j, c) for c in range(ngroups)] for j in tiles]
    for j, row in zip(tiles, sts):
        if mask_fn is not None:
            row = [mask_fn(st, j) for st in row]
        probs = []
        for st, (m, _), cq in zip(row, states, cqs):
            mt = jnp.max(st, axis=0, keepdims=True)
            if cq is not None:
                mt = mt + cq
            m_new = jnp.maximum(m, mt)
            p = jnp.exp2(st - (m_new if cq is None else m_new - cq))
            probs.append((m_new, jnp.exp2(m - m_new), p.astype(BF16)))
        vt1 = vt1_fn(j)
        states = tuple((m_new, alpha * acc + _dot(vt1, p)) for (m_new, alpha, p), (_, acc) in zip(probs, states))
    return states


def _causal_flash_t(nfull, logit_fn, mask_fn, vt1_fn, cqs, states):
    run = lambda tiles, s, mask=None: _flash_tiles_t(tiles, logit_fn, mask, vt1_fn, cqs, s)
    states = lax.fori_loop(0, nfull // 2, lambda i, s: run([2 * i, 2 * i + 1], s), states)
    states = lax.cond(nfull % 2 == 1, lambda s: run([nfull - 1], s), lambda s: s, states)
    return run([nfull], states, mask_fn)


def _normalised_t(acc, dv):
    return acc[:dv] / acc[dv:dv + 1]


def _causal_mask_t(st, key0, row0, tq):
    key = key0 + lax.broadcasted_iota(jnp.int32, st.shape, 0)
    row = row0 + lax.rem(lax.broadcasted_iota(jnp.int32, st.shape, 1), tq)
    return jnp.where(key <= row, st, MASKED)


def _fox_attn_kernel(q_ref, ka_ref, kb_ref, vt_ref, ct_ref, o_ref, *, tq, tk):
    qi = pl.program_id(2)
    hd = FOX_HEAD_DIM
    q = q_ref[0]
    cq = ct_ref[0]
    lane = lax.broadcasted_iota(jnp.int32, (tq, LANES), 1)

    def q_operand(gi):
        grp = q[:, (gi // 2) * LANES:(gi // 2 + 1) * LANES]
        base = FOX_BIAS_PIECES * gi + (FOX_BIAS_LANE0 if gi % 2 == 0 else 0)
        minus = jnp.where((lane >= base) & (lane < base + FOX_BIAS_PIECES), -1.0, 0.0).astype(q.dtype)
        return jnp.where((lane < hd) if gi % 2 == 0 else (lane >= hd), grp, minus)

    heads = [((ka_ref, kb_ref)[gi % 2], q_operand(gi), cq[gi:gi + 1] * LOG2E) for gi in range(FOX_GROUP)]

    tile = lambda j: pl.ds(pl.multiple_of(j * tk, tk), tk)
    logit_fn = lambda j, c: _dot_nt(heads[c][0][0, 0, tile(j), :], heads[c][1])
    mask_fn = lambda st, j: _causal_mask_t(st, j * tk, qi * tq, tq)
    vt1_fn = lambda j: vt_ref[0, :, tile(j)]
    init = tuple((jnp.full((1, tq), MASKED, F32), jnp.zeros((hd + ONES_ROWS, tq), F32)) for _ in heads)
    states = _causal_flash_t((qi * tq) // tk, logit_fn, mask_fn, vt1_fn, [cqx for _, _, cqx in heads], init)
    ot = jnp.concatenate([_normalised_t(acc, hd) for _, acc in states], axis=0)
    o_ref[0] = ot.T.astype(o_ref.dtype)


def _fox_prompt_attention(q, ka, kb, vt, ct, *, batch, seq_len):
    n = batch * seq_len
    tq = _tile(seq_len, 256)
    tk = _tile(seq_len, 512)
    assert tk % tq == 0 and FOX_GROUP == 4
    nq = seq_len // tq
    gw = FOX_GROUP * FOX_HEAD_DIM
    q3 = q.reshape(batch, seq_len, FOX_NQ)
    ka4 = ka.reshape(FOX_KV_HEADS, batch, seq_len, LANES)
    kb4 = kb.reshape(FOX_KV_HEADS, batch, seq_len, LANES)
    ct3 = ct.reshape(FOX_KV_HEADS, FOX_GROUP, n)
    kspec = pl.BlockSpec((1, 1, seq_len, LANES), lambda b, h, i: (h, b, 0, 0))
    o = pl.pallas_call(
        functools.partial(_fox_attn_kernel, tq=tq, tk=tk),
        grid=(batch, FOX_KV_HEADS, nq),
        in_specs=[
            pl.BlockSpec((1, tq, gw), lambda b, h, i: (b, i, h)),
            kspec, kspec,
            pl.BlockSpec((1, FOX_HEAD_DIM + ONES_ROWS, seq_len), lambda b, h, i: (h, 0, b)),
            pl.BlockSpec((1, FOX_GROUP, tq), lambda b, h, i: (h, 0, b * nq + i)),
        ],
        out_specs=pl.BlockSpec((1, tq, gw), lambda b, h, i: (b, i, h)),
        out_shape=jax.ShapeDtypeStruct((batch, seq_len, FOX_NQ), BF16),
        compiler_params=_params(("parallel", "parallel", "arbitrary")),
        name="fox_prompt_attn",
    )(q3, ka4, kb4, vt, ct3)
    return o.reshape(n, FOX_NQ)


def _page_copies(pt_ref, b, chunk, slot, ppc, pairs):
    out = []
    for i in range(ppc):
        page = pt_ref[b, chunk * ppc + i]
        for hbm, buf, sem in pairs:
            out.append(pltpu.make_async_copy(hbm.at[page], buf.at[slot, i], sem.at[slot]))
    return out


def _paged_chunk_step(pt_ref, ci, nchunks, ppc, pairs, reverse):
    b, nb = pl.program_id(0), pl.num_programs(0)
    order = (lambda c: nchunks - 1 - c) if reverse else (lambda c: c)
    slot = (b * nchunks + ci) % 2
    last = ci + 1 == nchunks
    nxt_b = jnp.where(last, b + 1, b)
    nxt_chunk = order(jnp.where(last, 0, ci + 1))

    @pl.when(jnp.logical_and(b == 0, ci == 0))
    def _():
        for cp in _page_copies(pt_ref, b, order(ci), slot, ppc, pairs):
            cp.start()

    @pl.when(jnp.logical_or(ci + 1 < nchunks, b + 1 < nb))
    def _():
        for cp in _page_copies(pt_ref, nxt_b, nxt_chunk, 1 - slot, ppc, pairs):
            cp.start()

    for cp in _page_copies(pt_ref, b, order(ci), slot, ppc, pairs):
        cp.wait()
    return order(ci), slot


def _pad_rows(x, rows):
    return jnp.concatenate([x, jnp.zeros((rows - x.shape[0], x.shape[1]), x.dtype)], axis=0)


def _suffix_sum_lanes(x):
    lane = lax.broadcasted_iota(jnp.int32, x.shape, 1)
    y = x
    sh = 1
    while sh < LANES:
        y = y + jnp.where(lane < LANES - sh, pltpu.roll(y, LANES - sh, axis=1), 0.0)
        sh *= 2
    return y


def _fox_sample_kernel(pt_ref, q_ref, kn_ref, vn_ref, dcol_ref, dtp_ref, kt_hbm, vt_hbm, lf_hbm,
                       o_ref, kbuf, vbuf, lbuf, ksem, vsem, lsem, *, ppc, nchunks, t_new):
    nk = ppc * PAGE_SIZE
    hd, g, heads = FOX_HEAD_DIM, FOX_GROUP, FOX_HEADS
    rows = heads * t_new
    pairs = ((kt_hbm, kbuf, ksem), (vt_hbm, vbuf, vsem), (lf_hbm, lbuf, lsem))

    q = q_ref[...]
    lane = lax.broadcasted_iota(jnp.int32, (t_new, FOX_NKV), 1)
    qrows = []
    for h in range(heads):
        kh, gi = divmod(h, g)
        grp = q[:, kh * FOX_NKV:(kh + 1) * FOX_NKV]
        shift = ((kh - gi) * hd) % FOX_NKV
        r = pltpu.roll(grp, shift, axis=1) if shift else grp
        qrows.append(jnp.where((lane >= kh * hd) & (lane < (kh + 1) * hd), r, 0.0))
    qbd = jnp.concatenate(qrows, axis=0).astype(BF16)
    dcol3 = dcol_ref[0].reshape(heads, t_new, 1)

    def body(ci, carry):
        m, l, acc, run = carry
        _, slot = _paged_chunk_step(pt_ref, ci, nchunks, ppc, pairs, reverse=True)
        s_pages = [None] * ppc
        for i in reversed(range(ppc)):
            lf = lbuf[slot, i]
            y = _suffix_sum_lanes(lf)
            bias = y - lf + run
            run = run + y[:, 0:1]
            s = _dot(qbd, kbuf[slot, i].astype(BF16))
            s_pages[i] = s.reshape(heads, t_new, PAGE_SIZE) + dcol3 + bias[:, None, :]
        pps = ppc // PAGE_GROUPS
        local = []
        for u in range(PAGE_GROUPS):
            mu, lu, pu = _local_softmax(jnp.concatenate(s_pages[u * pps:(u + 1) * pps], axis=2))
            local.append((mu, lu, pu.reshape(rows, pps * PAGE_SIZE).astype(BF16)))
        parts = []
        for u, (mu, lu, pu) in enumerate(local):
            ou = jnp.zeros((rows, FOX_NKV), F32)
            for i in range(pps):
                ou = ou + _dot_nt(pu[:, i * PAGE_SIZE:(i + 1) * PAGE_SIZE], vbuf[slot, u * pps + i].astype(BF16))
            parts.append((mu, lu, ou))
        return _merge_softmax(m, l, acc, parts) + (run,)

    init = (jnp.full((heads, t_new, 1), MASKED, F32), jnp.zeros((heads, t_new, 1), F32),
            jnp.zeros((rows, FOX_NKV), F32), jnp.zeros((heads, 1), F32))
    m, l, acc, _ = lax.fori_loop(0, nchunks, body, init)

    kn = _pad_rows(kn_ref[...], LANES).astype(BF16)
    vn = _pad_rows(vn_ref[...], LANES).astype(BF16)
    s3 = _dot_nt(qbd, kn).reshape(heads, t_new, LANES) + dcol3 - dtp_ref[0][:, None, :]
    trow = lax.broadcasted_iota(jnp.int32, (t_new, LANES), 0)
    tcol = lax.broadcasted_iota(jnp.int32, (t_new, LANES), 1)
    s3 = jnp.where((tcol <= trow)[None], s3, MASKED)
    m, l, acc = _online_softmax_step(s3, vn, m, l, acc)
    o = acc / l.reshape(rows, 1)

    groups = []
    for kh in range(FOX_KV_HEADS):
        tot = jnp.zeros((t_new, FOX_NKV), F32)
        for gi in range(g):
            h = kh * g + gi
            blk = o[h * t_new:(h + 1) * t_new]
            shift = ((gi - kh) * hd) % FOX_NKV
            r = pltpu.roll(blk, shift, axis=1) if shift else blk
            tot = tot + jnp.where((lane >= gi * hd) & (lane < (gi + 1) * hd), r, 0.0)
        groups.append(tot)
    o_ref[...] = jnp.concatenate(groups, axis=1)


def _fox_sample_attention(q, k_new, v_new, ct, page_table, kt_pages, vt_pages, lf_pages, *, t_new):
    ns = q.shape[0]
    bd, n_pages = page_table.shape
    assert bd * t_new == ns and t_new == SUBLANES
    ppc = _tile(n_pages, 32)
    nchunks = n_pages // ppc
    rows = FOX_HEADS * t_new
    d_bht = ct.reshape(FOX_HEADS, bd, t_new).transpose(1, 0, 2)
    dcol = d_bht.reshape(bd, rows, 1)
    dtp = jnp.pad(d_bht, ((0, 0), (0, 0), (0, LANES - t_new)))
    anyspec = pl.BlockSpec(memory_space=pl.ANY)
    return pl.pallas_call(
        functools.partial(_fox_sample_kernel, ppc=ppc, nchunks=nchunks, t_new=t_new),
        grid_spec=pltpu.PrefetchScalarGridSpec(
            num_scalar_prefetch=1,
            grid=(bd,),
            in_specs=[
                pl.BlockSpec((t_new, FOX_NQ), lambda b, pt: (b, 0)),
                pl.BlockSpec((t_new, FOX_NKV), lambda b, pt: (b, 0)),
                pl.BlockSpec((t_new, FOX_NKV), lambda b, pt: (b, 0)),
                pl.BlockSpec((1, rows, 1), lambda b, pt: (b, 0, 0)),
                pl.BlockSpec((1, FOX_HEADS, LANES), lambda b, pt: (b, 0, 0)),
                anyspec, anyspec, anyspec,
            ],
            out_specs=pl.BlockSpec((t_new, FOX_NQ), lambda b, pt: (b, 0)),
            scratch_shapes=[pltpu.VMEM((2, ppc, FOX_NKV, PAGE_SIZE), F32),
                            pltpu.VMEM((2, ppc, FOX_NKV, PAGE_SIZE), F32),
                            pltpu.VMEM((2, ppc, FOX_HEADS, PAGE_SIZE), F32),
                            pltpu.SemaphoreType.DMA((2,)),
                            pltpu.SemaphoreType.DMA((2,)),
                            pltpu.SemaphoreType.DMA((2,))],
        ),
        out_shape=jax.ShapeDtypeStruct((ns, FOX_NQ), F32),
        compiler_params=_params(("arbitrary",)),
        name="fox_sample_attn",
    )(page_table, q, k_new, v_new, dcol, dtp, kt_pages, vt_pages, lf_pages)


def _oproj_kernel(o_ref, wo_ref, x_ref, g_ref, out_ref):
    y = _dot(o_ref[...].astype(BF16), wo_ref[...])
    out_ref[...] = x_ref[...] + _rms(y, g_ref[...])


def _out_project(o, w_o, x, g):
    n, d = x.shape
    e = o.shape[1]
    tm = _tile(n, 512)
    row = lambda i: (i, 0)
    const = lambda i: (0, 0)
    return pl.pallas_call(
        _oproj_kernel,
        grid=(n // tm,),
        in_specs=[pl.BlockSpec((tm, e), row), pl.BlockSpec((e, d), const),
                  pl.BlockSpec((tm, d), row), pl.BlockSpec((1, d), const)],
        out_specs=pl.BlockSpec((tm, d), row),
        out_shape=jax.ShapeDtypeStruct((n, d), F32),
        compiler_params=_params(("parallel",)),
        name="out_proj",
    )(o, w_o.astype(BF16), x, g.reshape(1, d))


def _mla_oproj_kernel(ol_ref, wuv_ref, wo_ref, x_ref, g_ref, out_ref):
    parts = []
    for j in range(MLA_HEADS // 2):
        pair = jnp.concatenate([ol_ref[2 * j], ol_ref[2 * j + 1]], axis=1).astype(BF16)
        parts.append(_dot(pair, wuv_ref[j]).astype(BF16))
    y = _dot(jnp.concatenate(parts, axis=1), wo_ref[...])
    out_ref[...] = x_ref[...] + _rms(y, g_ref[...])


def _mla_out_project(o_lat, w_uv, w_o, x, g):
    n, d = x.shape
    tm = _tile(n, 512)
    npair = MLA_HEADS // 2
    wp = w_uv.reshape(MLA_KV_LORA, npair, 2, MLA_V_DIM).transpose(1, 2, 0, 3)
    z = jnp.zeros_like(wp[:, 0])
    wbd = jnp.concatenate([jnp.concatenate([wp[:, 0], z], axis=2),
                           jnp.concatenate([z, wp[:, 1]], axis=2)], axis=1).astype(BF16)
    row = lambda i: (i, 0)
    const = lambda i: (0, 0)
    return pl.pallas_call(
        _mla_oproj_kernel,
        grid=(n // tm,),
        in_specs=[pl.BlockSpec((MLA_HEADS, tm, MLA_KV_LORA), lambda i: (0, i, 0)),
                  pl.BlockSpec(wbd.shape, lambda i: (0, 0, 0)),
                  pl.BlockSpec(w_o.shape, const),
                  pl.BlockSpec((tm, d), row), pl.BlockSpec((1, d), const)],
        out_specs=pl.BlockSpec((tm, d), row),
        out_shape=jax.ShapeDtypeStruct((n, d), F32),
        compiler_params=_params(("parallel",)),
        name="mla_out_proj",
    )(o_lat, wbd, w_o.astype(BF16), x, g.reshape(1, d))


def _mla_proj_kernel(x_ref, g_ref, wa_ref, qn_ref, kvn_ref, wnope_ref, wpe_ref, wper_ref, wuk_ref,
                     cos_ref, sin_ref, qp_ref, kvp_ref, ckv_ref, ckvt_ref, kpe_ref, *, logit_scale):
    tm = x_ref.shape[0]
    scale = (MLA_NOPE + MLA_ROPE) ** -0.5 * logit_scale
    h = _rms(x_ref[...], g_ref[...]).astype(BF16)
    a = _dot(h, wa_ref[...])
    cos, sin = cos_ref[...], sin_ref[...]
    c_q = _rms(a[:, :MLA_Q_LORA], qn_ref[...]).astype(BF16)
    c_kv = _rms(a[:, MLA_Q_LORA:MLA_Q_LORA + MLA_KV_LORA], kvn_ref[...])
    r0 = MLA_Q_LORA + MLA_KV_LORA
    kpe_rep = a[:, r0:r0 + LANES] * cos + a[:, r0 + LANES:r0 + 2 * LANES] * sin
    ckv_ref[...] = c_kv
    ckvt_ref[0:MLA_KV_LORA, :] = c_kv.T.astype(BF16)
    ckvt_ref[MLA_KV_LORA:MLA_KV_LORA + ONES_ROWS, :] = jnp.ones((ONES_ROWS, tm), BF16)
    kpe_ref[...] = kpe_rep[:, :MLA_ROPE]
    kvp_ref[...] = jnp.concatenate([c_kv, kpe_rep], axis=1).astype(BF16)

    nrep = MLA_HEADS // ROPE_REP
    cos_all = jnp.concatenate([cos] * nrep, axis=1)
    sin_all = jnp.concatenate([sin] * nrep, axis=1)
    qpe = (_dot(c_q, wpe_ref[...]) * cos_all + _dot(c_q, wper_ref[...]) * sin_all) * scale
    qnope = _dot(c_q, wnope_ref[...]).astype(BF16)
    lane = lax.broadcasted_iota(jnp.int32, (tm, LANES), 1)
    for j in range(MLA_HEADS // 2):
        qlat2 = _dot(qnope[:, j * LANES:(j + 1) * LANES], wuk_ref[j]) * scale
        for w in range(2):
            hh = 2 * j + w
            grp = qpe[:, (hh // ROPE_REP) * LANES:(hh // ROPE_REP + 1) * LANES]
            quarter = hh % ROPE_REP
            pe = jnp.where((lane >= quarter * MLA_ROPE) & (lane < (quarter + 1) * MLA_ROPE), grp, 0.0)
            qp_ref[hh] = jnp.concatenate([qlat2[:, w * LANES:(w + 1) * LANES], pe], axis=1).astype(qp_ref.dtype)


def _rot_half_cols(w):
    half = MLA_ROPE // 2
    return jnp.concatenate([-w[..., half:], w[..., :half]], axis=-1)


def _mla_project(x, g, w_a, q_norm, kv_norm, w_uq, w_uk, pos, *, q_dtype, logit_scale):
    n, d = x.shape
    tm = _tile(n, 256)
    npos = pos.shape[0]
    assert n % npos == 0 and npos % tm == 0
    grid = (n // npos, npos // tm)
    g1 = grid[1]

    half = MLA_ROPE // 2
    inv = ROPE_THETA ** (-jnp.arange(half, dtype=F32) / half)
    ang = pos.astype(F32)[:, None] * inv[None, :]
    cos = jnp.tile(jnp.cos(ang), (1, 2 * ROPE_REP))
    sin = jnp.tile(jnp.sin(ang), (1, 2 * ROPE_REP))

    r0 = MLA_Q_LORA + MLA_KV_LORA
    w_pe = w_a[:, r0:]
    wa_ext = jnp.concatenate([w_a[:, :r0], jnp.tile(w_pe, (1, ROPE_REP)),
                              jnp.tile(_rot_half_cols(w_pe), (1, ROPE_REP))], axis=1).astype(BF16)
    w_nope = w_uq[:, :, :MLA_NOPE].reshape(MLA_Q_LORA, MLA_HEADS * MLA_NOPE).astype(BF16)
    w_qpe = w_uq[:, :, MLA_NOPE:]
    w_pe_q = w_qpe.reshape(MLA_Q_LORA, MLA_HEADS * MLA_ROPE).astype(BF16)
    w_pe_qr = _rot_half_cols(w_qpe).reshape(MLA_Q_LORA, MLA_HEADS * MLA_ROPE).astype(BF16)
    npair = MLA_HEADS // 2
    wk = w_uk.reshape(MLA_KV_LORA, npair, 2, MLA_NOPE).transpose(1, 2, 3, 0)
    z = jnp.zeros_like(wk[:, 0])
    wuk_bd = jnp.concatenate([jnp.concatenate([wk[:, 0], z], axis=2),
                              jnp.concatenate([z, wk[:, 1]], axis=2)], axis=1).astype(BF16)

    tok = lambda i, j: (i * g1 + j, 0)
    const = lambda i, j: (0, 0)
    posb = lambda i, j: (j, 0)
    return pl.pallas_call(
        functools.partial(_mla_proj_kernel, logit_scale=logit_scale),
        grid=grid,
        in_specs=[
            pl.BlockSpec((tm, d), tok),
            pl.BlockSpec((1, d), const),
            pl.BlockSpec(wa_ext.shape, const),
            pl.BlockSpec((1, MLA_Q_LORA), const),
            pl.BlockSpec((1, MLA_KV_LORA), const),
            pl.BlockSpec(w_nope.shape, const),
            pl.BlockSpec(w_pe_q.shape, const),
            pl.BlockSpec(w_pe_qr.shape, const),
            pl.BlockSpec(wuk_bd.shape, lambda i, j: (0, 0, 0)),
            pl.BlockSpec((tm, LANES), posb),
            pl.BlockSpec((tm, LANES), posb),
        ],
        out_specs=[
            pl.BlockSpec((MLA_HEADS, tm, MLA_QK), lambda i, j: (0, i * g1 + j, 0)),
            pl.BlockSpec((tm, MLA_QK), tok),
            pl.BlockSpec((tm, MLA_KV_LORA), tok),
            pl.BlockSpec((MLA_KV_LORA + ONES_ROWS, tm), lambda i, j: (0, i * g1 + j)),
            pl.BlockSpec((tm, MLA_ROPE), tok),
        ],
        out_shape=[
            jax.ShapeDtypeStruct((MLA_HEADS, n, MLA_QK), q_dtype),
            jax.ShapeDtypeStruct((n, MLA_QK), BF16),
            jax.ShapeDtypeStruct((n, MLA_KV_LORA), F32),
            jax.ShapeDtypeStruct((MLA_KV_LORA + ONES_ROWS, n), BF16),
            jax.ShapeDtypeStruct((n, MLA_ROPE), F32),
        ],
        compiler_params=_params(("parallel", "parallel")),
        name="mla_proj",
    )(x, g.reshape(1, d), wa_ext, q_norm.reshape(1, -1), kv_norm.reshape(1, -1),
      w_nope, w_pe_q, w_pe_qr, wuk_bd, cos, sin)


def _mla_attn_kernel(q_ref, kv_ref, ckvt_ref, o_ref, *, tq, tk):
    qi = pl.program_id(1)
    nh = MLA_HEADS
    ngroups = 4
    cols = nh * tq // ngroups
    q = q_ref[...].reshape(nh * tq, MLA_QK)
    qs = [q[i * cols:(i + 1) * cols] for i in range(ngroups)]
    tile = lambda j: pl.ds(pl.multiple_of(j * tk, tk), tk)
    logit_fn = lambda j, c: _dot_nt(kv_ref[0, tile(j), :], qs[c])
    mask_fn = lambda st, j: _causal_mask_t(st, j * tk, qi * tq, tq)
    vt1_fn = lambda j: ckvt_ref[:, tile(j)]
    init = tuple((jnp.full((1, cols), MASKED, F32), jnp.zeros((MLA_KV_LORA + ONES_ROWS, cols), F32))
                 for _ in range(ngroups))
    states = _causal_flash_t((qi * tq) // tk, logit_fn, mask_fn, vt1_fn, [None] * ngroups, init)
    ot = jnp.concatenate([_normalised_t(acc, MLA_KV_LORA) for _, acc in states], axis=1)
    o_ref[...] = ot.T.reshape(nh, tq, MLA_KV_LORA).astype(o_ref.dtype)


def _mla_prompt_attention(qp, kvp, ckvt, *, batch, seq_len):
    n = batch * seq_len
    tq = _tile(seq_len, 64)
    tk = _tile(seq_len, 512)
    assert tk % tq == 0
    nq = seq_len // tq
    kv3 = kvp.reshape(batch, seq_len, MLA_QK)
    return pl.pallas_call(
        functools.partial(_mla_attn_kernel, tq=tq, tk=tk),
        grid=(batch, nq),
        in_specs=[pl.BlockSpec((MLA_HEADS, tq, MLA_QK), lambda b, i: (0, b * nq + i, 0)),
                  pl.BlockSpec((1, seq_len, MLA_QK), lambda b, i: (b, 0, 0)),
                  pl.BlockSpec((MLA_KV_LORA + ONES_ROWS, seq_len), lambda b, i: (0, b))],
        out_specs=pl.BlockSpec((MLA_HEADS, tq, MLA_KV_LORA), lambda b, i: (0, b * nq + i, 0)),
        out_shape=jax.ShapeDtypeStruct((MLA_HEADS, n, MLA_KV_LORA), BF16),
        compiler_params=_params(("parallel", "arbitrary")),
        name="mla_prompt_attn",
    )(qp, kv3, ckvt)


def _mla_sample_kernel(pt_ref, q_ref, cn_ref, rn_ref, c_hbm, r_hbm, o_ref, cbuf, rbuf, csem, rsem,
                       *, ppc, nchunks, t_new):
    nk = ppc * PAGE_SIZE
    nh = MLA_HEADS
    rows = nh * t_new
    pairs = ((c_hbm, cbuf, csem), (r_hbm, rbuf, rsem))

    q = q_ref[...].reshape(rows, MLA_QK)
    qlat = q[:, :MLA_KV_LORA].astype(BF16)
    grp = q[:, MLA_KV_LORA:]
    qpe = grp[:, :MLA_ROPE]
    for i in range(1, ROPE_REP):
        qpe = qpe + grp[:, i * MLA_ROPE:(i + 1) * MLA_ROPE]
    qpe = qpe.astype(BF16)

    def body(c, carry):
        _, slot = _paged_chunk_step(pt_ref, c, nchunks, ppc, pairs, reverse=False)
        pps = ppc // PAGE_GROUPS
        nsub = pps * PAGE_SIZE
        logits = []
        for u in range(PAGE_GROUPS):
            ckv = cbuf[slot, u * pps:(u + 1) * pps].reshape(nsub, MLA_KV_LORA).astype(BF16)
            s_pe = jnp.concatenate([_dot(qpe, rbuf[slot, u * pps + i].astype(BF16)) for i in range(pps)], axis=1)
            logits.append((ckv, (_dot_nt(qlat, ckv) + s_pe).reshape(nh, t_new, nsub)))
        local = [(ckv,) + _local_softmax(s3) for ckv, s3 in logits]
        parts = [(mu, lu, _dot(pu.reshape(rows, nsub).astype(BF16), ckv)) for ckv, mu, lu, pu in local]
        return _merge_softmax(*carry, parts)

    init = (jnp.full((nh, t_new, 1), MASKED, F32), jnp.zeros((nh, t_new, 1), F32),
            jnp.zeros((rows, MLA_KV_LORA), F32))
    m, l, acc = lax.fori_loop(0, nchunks, body, init)

    cn = _pad_rows(cn_ref[...], LANES).astype(BF16)
    rn = _pad_rows(rn_ref[...], LANES).astype(BF16)
    s3 = (_dot_nt(qlat, cn) + _dot_nt(qpe, rn)).reshape(nh, t_new, LANES)
    trow = lax.broadcasted_iota(jnp.int32, (t_new, LANES), 0)
    tcol = lax.broadcasted_iota(jnp.int32, (t_new, LANES), 1)
    s3 = jnp.where((tcol <= trow)[None], s3, MASKED)
    m, l, acc = _online_softmax_step(s3, cn, m, l, acc)
    o_ref[...] = (acc / l.reshape(rows, 1)).reshape(nh, t_new, MLA_KV_LORA)


def _mla_sample_attention(qp, c_new, kpe_new, page_table, ckv_pages, kpet_pages, *, t_new):
    ns = c_new.shape[0]
    bd, n_pages = page_table.shape
    assert bd * t_new == ns and t_new == SUBLANES
    ppc = _tile(n_pages, 32)
    nchunks = n_pages // ppc
    return pl.pallas_call(
        functools.partial(_mla_sample_kernel, ppc=ppc, nchunks=nchunks, t_new=t_new),
        grid_spec=pltpu.PrefetchScalarGridSpec(
            num_scalar_prefetch=1,
            grid=(bd,),
            in_specs=[
                pl.BlockSpec((MLA_HEADS, t_new, MLA_QK), lambda b, pt: (0, b, 0)),
                pl.BlockSpec((t_new, MLA_KV_LORA), lambda b, pt: (b, 0)),
                pl.BlockSpec((t_new, MLA_ROPE), lambda b, pt: (b, 0)),
                pl.BlockSpec(memory_space=pl.ANY),
                pl.BlockSpec(memory_space=pl.ANY),
            ],
            out_specs=pl.BlockSpec((MLA_HEADS, t_new, MLA_KV_LORA), lambda b, pt: (0, b, 0)),
            scratch_shapes=[pltpu.VMEM((2, ppc, PAGE_SIZE, MLA_KV_LORA), F32),
                            pltpu.VMEM((2, ppc, MLA_ROPE, PAGE_SIZE), F32),
                            pltpu.SemaphoreType.DMA((2,)),
                            pltpu.SemaphoreType.DMA((2,))],
        ),
        out_shape=jax.ShapeDtypeStruct((MLA_HEADS, ns, MLA_KV_LORA), F32),
        compiler_params=_params(("arbitrary",)),
        name="mla_sample_attn",
    )(page_table, qp, c_new, kpe_new, ckv_pages, kpet_pages)


def _ffn_kernel(x_ref, g1_ref, g2_ref, win_ref, cw_ref, cb_ref, wout_ref, *rest, fc, seq_rows, chained):
    if chained:
        out_ref, st_out_ref, act_ref, carry_ref = rest
    else:
        st_ref, out_ref, st_out_ref, act_ref = rest
    tm, d = x_ref.shape
    dff = wout_ref.shape[0]
    x = x_ref[...]
    h = _rms(x, g1_ref[...]).astype(BF16)

    if chained:
        @pl.when(pl.program_id(1) == 0)
        def _():
            carry_ref[...] = jnp.zeros_like(carry_ref)
        row = lax.broadcasted_iota(jnp.int32, (tm, fc), 0)
    else:
        nseq = tm // seq_rows
        row = lax.broadcasted_iota(jnp.int32, (nseq, seq_rows, fc), 1)

    for f in range(dff // fc):
        cs = slice(f * fc, (f + 1) * fc)
        gate = _dot(h, win_ref[:, cs])
        up = _dot(h, win_ref[:, dff + f * fc:dff + (f + 1) * fc])
        w0, w1, w2 = cw_ref[0:1, cs], cw_ref[1:2, cs], cw_ref[2:3, cs]
        if chained:
            p0, p1 = carry_ref[0:1, cs], carry_ref[1:2, cs]
            sh1 = jnp.where(row == 0, p1, pltpu.roll(gate, 1, axis=0))
            sh2 = jnp.where(row == 0, p0, jnp.where(row == 1, p1, pltpu.roll(gate, 2, axis=0)))
            carry_ref[0:2, cs] = gate[tm - 2:tm]
            st_out_ref[0, :, cs] = gate[tm - 2:tm]
            conv = cb_ref[:, cs] + w0 * sh2 + w1 * sh1 + w2 * gate
        else:
            g3 = gate.reshape(nseq, seq_rows, fc)
            p0, p1 = st_ref[:, 0:1, cs], st_ref[:, 1:2, cs]
            sh1 = jnp.where(row == 0, p1, pltpu.roll(g3, 1, axis=1))
            sh2 = jnp.where(row == 0, p0, jnp.where(row == 1, p1, pltpu.roll(g3, 2, axis=1)))
            st_out_ref[:, :, cs] = g3[:, seq_rows - 2:seq_rows, :]
            conv = (cb_ref[:, cs] + w0 * sh2 + w1 * sh1 + w2 * g3).reshape(tm, fc)
        act_ref[:, cs] = (jax.nn.gelu(conv, approximate=True) * up).astype(BF16)

    y = _dot(act_ref[...], wout_ref[...])
    out_ref[...] = x + _rms(y, g2_ref[...])


def _conv_ffn_block(x, g1, g2, w_in, conv_w, conv_b, w_out, state, *, seq_len):
    n, d = x.shape
    dff = w_out.shape[0]
    fc = 256
    assert dff % fc == 0 and CONV_W == 3
    nseq_total = n // seq_len
    chained = state is None
    wspec = dict(pipeline_mode=pl.Buffered(1))
    if chained:
        tm = _tile(seq_len, 512)
        grid = (nseq_total, seq_len // tm)
        g1n = grid[1]
        tok = lambda i, j: (i * g1n + j, 0)
        const = lambda i, j: (0, 0)
        st_spec = pl.BlockSpec((1, CONV_W - 1, dff), lambda i, j: (i, 0, 0))
        extra_in, extra_specs = [], []
        scratch = [pltpu.VMEM((tm, dff), BF16), pltpu.VMEM((SUBLANES, dff), F32)]
        sem = ("arbitrary", "arbitrary")
    else:
        sb = _tile(nseq_total, 32)
        tm = sb * seq_len
        grid = (nseq_total // sb,)
        tok = lambda i: (i, 0)
        const = lambda i: (0, 0)
        st_spec = pl.BlockSpec((sb, CONV_W - 1, dff), lambda i: (i, 0, 0))
        extra_in, extra_specs = [state], [st_spec]
        scratch = [pltpu.VMEM((tm, dff), BF16)]
        sem = ("arbitrary",)
    out, st_new = pl.pallas_call(
        functools.partial(_ffn_kernel, fc=fc, seq_rows=seq_len, chained=chained),
        grid=grid,
        in_specs=[
            pl.BlockSpec((tm, d), tok),
            pl.BlockSpec((1, d), const),
            pl.BlockSpec((1, d), const),
            pl.BlockSpec((d, 2 * dff), const, **wspec),
            pl.BlockSpec((CONV_W, dff), const),
            pl.BlockSpec((1, dff), const),
            pl.BlockSpec((dff, d), const, **wspec),
        ] + extra_specs,
        out_specs=[pl.BlockSpec((tm, d), tok), st_spec],
        out_shape=[jax.ShapeDtypeStruct((n, d), F32),
                   jax.ShapeDtypeStruct((nseq_total, CONV_W - 1, dff), F32)],
        scratch_shapes=scratch,
        compiler_params=_params(sem),
        name="conv_ffn",
    )(x, g1.reshape(1, d), g2.reshape(1, d), w_in.astype(BF16), conv_w, conv_b.reshape(1, dff),
      w_out.astype(BF16), *extra_in)
    return out, st_new


def kernel(x_prompt, x_sample, cache_fox_k, cache_fox_v, cache_fox_logf, cache_mla_ckv, cache_mla_kpe, state_conv, page_table, norm_mix_pre, norm_mix_post, norm_ffn_pre, norm_ffn_post, fox_w_in, fox_b_f, fox_w_o, mla_w_a, mla_q_norm, mla_kv_norm, mla_w_uq, mla_w_uk, mla_w_uv, mla_w_o, ffn_w_in, ffn_conv_w, ffn_conv_b, ffn_w_out):
    batch, seq_len, d = x_prompt.shape
    bd, t_new, _ = x_sample.shape
    depth = norm_mix_pre.shape[0]
    n_pool = cache_fox_k.shape[1]
    past = page_table.shape[1] * PAGE_SIZE
    xp = x_prompt.reshape(batch * seq_len, d)
    xs = x_sample.reshape(bd * t_new, d)

    fox_kt_pages = jnp.transpose(cache_fox_k, (0, 1, 3, 4, 2)).reshape(-1, FOX_NKV, PAGE_SIZE)
    fox_vt_pages = jnp.transpose(cache_fox_v, (0, 1, 3, 4, 2)).reshape(-1, FOX_NKV, PAGE_SIZE)
    fox_f_pages = jnp.transpose(cache_fox_logf, (0, 1, 3, 2)).reshape(-1, FOX_HEADS, PAGE_SIZE)
    mla_c_pages = cache_mla_ckv.reshape(-1, PAGE_SIZE, MLA_KV_LORA)
    mla_rt_pages = jnp.transpose(cache_mla_kpe, (0, 1, 3, 2)).reshape(-1, MLA_ROPE, PAGE_SIZE)

    outs_p = {k: [] for k in ("fk", "fv", "ff", "mc", "mr", "cv")}
    outs_s = {k: [] for k in ("fk", "fv", "ff", "mc", "mr", "cv")}
    for i in range(depth):
        li = i // N_MIXERS
        pages = page_table + li * n_pool
        if i % N_MIXERS == 0:
            w_in, b_f, w_o = fox_w_in[li], fox_b_f[li], fox_w_o[li]
            q, _, _, ktf, vtf, ka, kb, vt, logf, ct = _fox_project(xp, norm_mix_pre[i], w_in, b_f, seq_len=seq_len,
                                                                   q_dtype=BF16, logit_scale=LOG2E)
            o = _fox_prompt_attention(q, ka, kb, vt, ct, batch=batch, seq_len=seq_len)
            xp = _out_project(o, w_o, xp, norm_mix_post[i])
            heads_t = lambda a: a.reshape(batch, FOX_KV_HEADS, FOX_HEAD_DIM, seq_len).transpose(0, 3, 1, 2)
            outs_p["fk"].append(heads_t(ktf))
            outs_p["fv"].append(heads_t(vtf))
            outs_p["ff"].append(logf.reshape(batch, seq_len, FOX_HEADS))

            q, k, v, _, _, _, _, _, logf, ct = _fox_project(xs, norm_mix_pre[i], w_in, b_f, seq_len=t_new,
                                                            q_dtype=F32, logit_scale=1.0)
            o = _fox_sample_attention(q, k, v, ct, pages, fox_kt_pages, fox_vt_pages, fox_f_pages, t_new=t_new)
            xs = _out_project(o, w_o, xs, norm_mix_post[i])
            outs_s["fk"].append(k.reshape(bd, t_new, FOX_KV_HEADS, FOX_HEAD_DIM))
            outs_s["fv"].append(v.reshape(bd, t_new, FOX_KV_HEADS, FOX_HEAD_DIM))
            outs_s["ff"].append(logf.reshape(bd, t_new, FOX_HEADS))
        else:
            args = (mla_w_a[li], mla_q_norm[li], mla_kv_norm[li], mla_w_uq[li], mla_w_uk[li])
            qp, kvp, ckv, ckvt, kpe = _mla_project(xp, norm_mix_pre[i], *args, jnp.arange(seq_len),
                                                   q_dtype=BF16, logit_scale=LOG2E)
            o_lat = _mla_prompt_attention(qp, kvp, ckvt, batch=batch, seq_len=seq_len)
            xp = _mla_out_project(o_lat, mla_w_uv[li], mla_w_o[li], xp, norm_mix_post[i])
            outs_p["mc"].append(ckv.reshape(batch, seq_len, MLA_KV_LORA))
            outs_p["mr"].append(kpe.reshape(batch, seq_len, MLA_ROPE))

            pos_s = jnp.tile(past + jnp.arange(t_new), bd)
            qp, _, ckv, _, kpe = _mla_project(xs, norm_mix_pre[i], *args, pos_s, q_dtype=F32, logit_scale=1.0)
            o_lat = _mla_sample_attention(qp, ckv, kpe, pages, mla_c_pages, mla_rt_pages, t_new=t_new)
            xs = _mla_out_project(o_lat, mla_w_uv[li], mla_w_o[li], xs, norm_mix_post[i])
            outs_s["mc"].append(ckv.reshape(bd, t_new, MLA_KV_LORA))
            outs_s["mr"].append(kpe.reshape(bd, t_new, MLA_ROPE))

        ffn = (norm_ffn_pre[i], norm_ffn_post[i], ffn_w_in[i], ffn_conv_w[i], ffn_conv_b[i], ffn_w_out[i])
        xp, cvp = _conv_ffn_block(xp, *ffn, None, seq_len=seq_len)
        xs, cvs = _conv_ffn_block(xs, *ffn, state_conv[i], seq_len=t_new)
        outs_p["cv"].append(cvp)
        outs_s["cv"].append(cvs)

    ldt = cache_fox_logf.dtype
    st = jnp.stack
    return (xp.reshape(batch, seq_len, d), xs.reshape(bd, t_new, d),
            st(outs_p["fk"]), st(outs_p["fv"]), st(outs_p["ff"]).astype(ldt), st(outs_p["mc"]), st(outs_p["mr"]), st(outs_p["cv"]),
            st(outs_s["fk"]), st(outs_s["fv"]), st(outs_s["ff"]).astype(ldt), st(outs_s["mc"]), st(outs_s["mr"]), st(outs_s["cv"]))
```

```python
import functools

import jax
import jax.numpy as jnp
from jax import lax
from jax.experimental import pallas as pl
from jax.experimental.pallas import tpu as pltpu

F32 = jnp.float32
BF16 = jnp.bfloat16

N_MIXERS = 2
FOX_HEADS = 16
FOX_KV_HEADS = 4
FOX_GROUP = FOX_HEADS // FOX_KV_HEADS
FOX_HEAD_DIM = 64
FOX_NQ = FOX_HEADS * FOX_HEAD_DIM
FOX_NKV = FOX_KV_HEADS * FOX_HEAD_DIM
MLA_HEADS = 16
MLA_Q_LORA = 256
MLA_KV_LORA = 128
MLA_NOPE = 64
MLA_ROPE = 32
MLA_V_DIM = 64
ROPE_THETA = 10000.0
CONV_W = 3
PAGE_SIZE = 128
EPS = 1e-6

LANES = 128
SUBLANES = 8
V7X_VMEM_LIMIT_BYTES = 56 * 1024 * 1024

MASKED = -1e30
MLA_QK = 2 * LANES
ROPE_REP = LANES // MLA_ROPE

LOG2E = 1.4426950408889634
ONES_ROWS = 16
PAGE_GROUPS = 4
FOX_BIAS_PIECES = 3
FOX_BIAS_LANE0 = FOX_HEAD_DIM

NT_DIMS = (((1,), (1,)), ((), ()))


def _params(sem):
    return pltpu.CompilerParams(dimension_semantics=sem, vmem_limit_bytes=V7X_VMEM_LIMIT_BYTES)


def _tile(n, pref):
    t = min(n, pref)
    assert n % t == 0, (n, pref)
    return t


def _rms(x, g):
    return x * lax.rsqrt(jnp.mean(x * x, axis=-1, keepdims=True) + EPS) * g


def _log_sigmoid(x):
    return jnp.minimum(x, 0.0) - jnp.log1p(jnp.exp(-jnp.abs(x)))


def _split3(x):
    hi = x.astype(BF16)
    r1 = x - hi.astype(F32)
    mid = r1.astype(BF16)
    lo = (r1 - mid.astype(F32)).astype(BF16)
    return hi, mid, lo


def _dot(a, b):
    return jnp.dot(a, b, preferred_element_type=F32)


def _dot_nt(a, b):
    return lax.dot_general(a, b, NT_DIMS, preferred_element_type=F32)


def _fox_proj_kernel(x_ref, g_ref, w_ref, wft_ref, bf_ref, bft_ref, u_ref, l_ref, pa_ref, pb_ref,
                     q_ref, k_ref, v_ref, ktf_ref, vtf_ref, ka_ref, kb_ref, vt_ref, logf_ref, ct_ref,
                     carry_ref, carryt_ref, *, logit_scale):
    tm = x_ref.shape[0]
    hd = FOX_HEAD_DIM

    @pl.when(pl.program_id(1) == 0)
    def _():
        carry_ref[...] = jnp.zeros_like(carry_ref)
        carryt_ref[...] = jnp.zeros_like(carryt_ref)

    h = _rms(x_ref[...], g_ref[...]).astype(BF16)
    proj = _dot(h, w_ref[...])
    q_ref[...] = (proj[:, :FOX_NQ] * (hd ** -0.5 * logit_scale)).astype(q_ref.dtype)
    k = proj[:, FOX_NQ:FOX_NQ + FOX_NKV]
    v = proj[:, FOX_NQ + FOX_NKV:FOX_NQ + 2 * FOX_NKV]
    k_ref[...] = k
    v_ref[...] = v
    ktf_ref[0] = k.T
    vtf_ref[0] = v.T
    vt = vtf_ref[0].astype(BF16)
    for kh in range(FOX_KV_HEADS):
        vt_ref[kh, 0:hd, :] = vt[kh * hd:(kh + 1) * hd]
        vt_ref[kh, hd:hd + ONES_ROWS, :] = jnp.ones((ONES_ROWS, tm), BF16)
    f0 = FOX_NQ + 2 * FOX_NKV
    logf = _log_sigmoid(proj[:, f0:f0 + FOX_HEADS] + bf_ref[...])
    logf_ref[...] = logf
    lmat = l_ref[...]
    c = carry_ref[...]
    for piece in _split3(logf):
        c = c + _dot(lmat, piece)
    carry_ref[...] = c[tm - 1:tm, :]
    placed_a = jnp.zeros((tm, FOX_KV_HEADS * LANES), F32)
    placed_b = jnp.zeros((tm, FOX_KV_HEADS * LANES), F32)
    for p, piece in enumerate(_split3(c * logit_scale)):
        placed_a = placed_a + _dot(piece, pa_ref[p])
        placed_b = placed_b + _dot(piece, pb_ref[p])
    lane = lax.broadcasted_iota(jnp.int32, (tm, LANES), 1)
    for kh in range(FOX_KV_HEADS):
        grp = k[:, (kh // 2) * LANES:(kh // 2 + 1) * LANES]
        swapped = pltpu.roll(grp, hd, axis=1)
        low, high = (grp, swapped) if kh % 2 == 0 else (swapped, grp)
        sl = slice(kh * LANES, (kh + 1) * LANES)
        ka_ref[kh] = jnp.where(lane < hd, low, placed_a[:, sl]).astype(BF16)
        kb_ref[kh] = jnp.where(lane >= hd, high, placed_b[:, sl]).astype(BF16)
    logft = _log_sigmoid(_dot_nt(wft_ref[...], h) + bft_ref[...])
    umat = u_ref[...]
    ct = carryt_ref[...]
    for piece in _split3(logft):
        ct = ct + _dot(piece, umat)
    ct_ref[...] = ct
    carryt_ref[...] = ct[:, tm - 1:tm]


def _fox_project(x, g, w_in, b_f, *, seq_len, q_dtype, logit_scale):
    n, d = x.shape
    tm = _tile(n, 512)
    if seq_len >= tm:
        assert seq_len % tm == 0
        grid = (n // seq_len, seq_len // tm)
        grp = jnp.zeros((tm,), jnp.int32)
    else:
        assert tm % seq_len == 0
        grid = (n // tm, 1)
        grp = jnp.arange(tm, dtype=jnp.int32) // seq_len
    g1 = grid[1]
    pos = jnp.arange(tm, dtype=jnp.int32)
    same = grp[:, None] == grp[None, :]
    umat = (same & (pos[:, None] <= pos[None, :])).astype(BF16)
    lmat = umat.T
    f0 = FOX_NQ + 2 * FOX_NKV
    w_main = jnp.pad(w_in, ((0, 0), (0, LANES - FOX_HEADS))).astype(BF16)
    wft = w_in[:, f0:].T.astype(BF16)
    wcols = w_main.shape[1]
    hh = jnp.arange(FOX_HEADS)
    pp = jnp.arange(FOX_BIAS_PIECES)
    col_b = (hh // FOX_GROUP)[None, :] * LANES + FOX_BIAS_PIECES * (hh % FOX_GROUP)[None, :] + pp[:, None]
    cols = jnp.arange(FOX_KV_HEADS * LANES)
    place_b = (col_b[:, :, None] == cols[None, None, :]).astype(BF16)
    place_a = (col_b[:, :, None] + FOX_HEAD_DIM == cols[None, None, :]).astype(BF16)

    tok = lambda i, j: (i * g1 + j, 0)
    tok3 = lambda i, j: (0, i * g1 + j, 0)
    tokt = lambda i, j: (0, i * g1 + j)
    const = lambda i, j: (0, 0)
    const3 = lambda i, j: (0, 0, 0)
    outs = pl.pallas_call(
        functools.partial(_fox_proj_kernel, logit_scale=logit_scale),
        grid=grid,
        in_specs=[
            pl.BlockSpec((tm, d), tok),
            pl.BlockSpec((1, d), const),
            pl.BlockSpec((d, wcols), const),
            pl.BlockSpec((FOX_HEADS, d), const),
            pl.BlockSpec((1, FOX_HEADS), const),
            pl.BlockSpec((FOX_HEADS, 1), const),
            pl.BlockSpec((tm, tm), const),
            pl.BlockSpec((tm, tm), const),
            pl.BlockSpec(place_a.shape, const3),
            pl.BlockSpec(place_b.shape, const3),
        ],
        out_specs=[
            pl.BlockSpec((tm, FOX_NQ), tok),
            pl.BlockSpec((tm, FOX_NKV), tok),
            pl.BlockSpec((tm, FOX_NKV), tok),
            pl.BlockSpec((1, FOX_NKV, tm), lambda i, j: (i, 0, j)),
            pl.BlockSpec((1, FOX_NKV, tm), lambda i, j: (i, 0, j)),
            pl.BlockSpec((FOX_KV_HEADS, tm, LANES), tok3),
            pl.BlockSpec((FOX_KV_HEADS, tm, LANES), tok3),
            pl.BlockSpec((FOX_KV_HEADS, FOX_HEAD_DIM + ONES_ROWS, tm), lambda i, j: (0, 0, i * g1 + j)),
            pl.BlockSpec((tm, FOX_HEADS), tok),
            pl.BlockSpec((FOX_HEADS, tm), tokt),
        ],
        out_shape=[
            jax.ShapeDtypeStruct((n, FOX_NQ), q_dtype),
            jax.ShapeDtypeStruct((n, FOX_NKV), F32),
            jax.ShapeDtypeStruct((n, FOX_NKV), F32),
            jax.ShapeDtypeStruct((grid[0], FOX_NKV, g1 * tm), F32),
            jax.ShapeDtypeStruct((grid[0], FOX_NKV, g1 * tm), F32),
            jax.ShapeDtypeStruct((FOX_KV_HEADS, n, LANES), BF16),
            jax.ShapeDtypeStruct((FOX_KV_HEADS, n, LANES), BF16),
            jax.ShapeDtypeStruct((FOX_KV_HEADS, FOX_HEAD_DIM + ONES_ROWS, n), BF16),
            jax.ShapeDtypeStruct((n, FOX_HEADS), F32),
            jax.ShapeDtypeStruct((FOX_HEADS, n), F32),
        ],
        scratch_shapes=[pltpu.VMEM((1, FOX_HEADS), F32), pltpu.VMEM((FOX_HEADS, 1), F32)],
        compiler_params=_params(("arbitrary", "arbitrary")),
        name="fox_proj",
    )(x, g.reshape(1, d), w_main, wft, b_f.reshape(1, FOX_HEADS), b_f.reshape(FOX_HEADS, 1), umat, lmat,
      place_a, place_b)
    return outs


def _online_softmax_step(s, v, m, l, acc):
    rows, n = acc.shape[0], s.shape[-1]
    m_new = jnp.maximum(m, jnp.max(s, axis=-1, keepdims=True))
    alpha = jnp.exp(m - m_new)
    p = jnp.exp(s - m_new)
    l_new = alpha * l + jnp.sum(p, axis=-1, keepdims=True)
    pv = _dot(p.reshape(rows, n).astype(BF16), v)
    return m_new, l_new, alpha.reshape(rows, 1) * acc + pv


def _local_softmax(s):
    m = jnp.max(s, axis=-1, keepdims=True)
    p = jnp.exp(s - m)
    return m, jnp.sum(p, axis=-1, keepdims=True), p


def _merge_softmax(m, l, acc, parts):
    rows = acc.shape[0]
    m_new = m
    for mu, _, _ in parts:
        m_new = jnp.maximum(m_new, mu)
    w = jnp.exp(m - m_new)
    l_new, acc_new = w * l, w.reshape(rows, 1) * acc
    for mu, lu, ou in parts:
        w = jnp.exp(mu - m_new)
        l_new, acc_new = l_new + w * lu, acc_new + w.reshape(rows, 1) * ou
    return m_new, l_new, acc_new


def _flash_tiles_t(tiles, logit_fn, mask_fn, vt1_fn, cqs, states):
    ngroups = len(states)
    sts = [[logit_fn(j, c) for c in range(ngroups)] for j in tiles]
    for j, row in zip(tiles, sts):
        if mask_fn is not None:
            row = [mask_fn(st, j) for st in row]
        probs = []
        for st, (m, _), cq in zip(row, states, cqs):
            mt = jnp.max(st, axis=0, keepdims=True)
            if cq is not None:
                mt = mt + cq
            m_new = jnp.maximum(m, mt)
            p = jnp.exp2(st - (m_new if cq is None else m_new - cq))
            probs.append((m_new, jnp.exp2(m - m_new), p.astype(BF16)))
        vt1 = vt1_fn(j)
        states = tuple((m_new, alpha * acc + _dot(vt1, p)) for (m_new, alpha, p), (_, acc) in zip(probs, states))
    return states


def _causal_flash_t(nfull, logit_fn, mask_fn, vt1_fn, cqs, states):
    run = lambda tiles, s, mask=None: _flash_tiles_t(tiles, logit_fn, mask, vt1_fn, cqs, s)
    states = lax.fori_loop(0, nfull // 2, lambda i, s: run([2 * i, 2 * i + 1], s), states)
    states = lax.cond(nfull % 2 == 1, lambda s: run([nfull - 1], s), lambda s: s, states)
    return run([nfull], states, mask_fn)


def _normalised_t(acc, dv):
    return acc[:dv] / acc[dv:dv + 1]


def _causal_mask_t(st, key0, row0, tq):
    key = key0 + lax.broadcasted_iota(jnp.int32, st.shape, 0)
    row = row0 + lax.rem(lax.broadcasted_iota(jnp.int32, st.shape, 1), tq)
    return jnp.where(key <= row, st, MASKED)


def _fox_attn_kernel(q_ref, ka_ref, kb_ref, vt_ref, ct_ref, o_ref, *, tq, tk):
    qi = pl.program_id(2)
    hd = FOX_HEAD_DIM
    q = q_ref[0]
    cq = ct_ref[0]
    lane = lax.broadcasted_iota(jnp.int32, (tq, LANES), 1)

    def q_operand(gi):
        grp = q[:, (gi // 2) * LANES:(gi // 2 + 1) * LANES]
        base = FOX_BIAS_PIECES * gi + (FOX_BIAS_LANE0 if gi % 2 == 0 else 0)
        minus = jnp.where((lane >= base) & (lane < base + FOX_BIAS_PIECES), -1.0, 0.0).astype(q.dtype)
        return jnp.where((lane < hd) if gi % 2 == 0 else (lane >= hd), grp, minus)

    heads = [((ka_ref, kb_ref)[gi % 2], q_operand(gi), cq[gi:gi + 1] * LOG2E) for gi in range(FOX_GROUP)]

    tile = lambda j: pl.ds(pl.multiple_of(j * tk, tk), tk)
    logit_fn = lambda j, c: _dot_nt(heads[c][0][0, 0, tile(j), :], heads[c][1])
    mask_fn = lambda st, j: _causal_mask_t(st, j * tk, qi * tq, tq)
    vt1_fn = lambda j: vt_ref[0, :, tile(j)]
    init = tuple((jnp.full((1, tq), MASKED, F32), jnp.zeros((hd + ONES_ROWS, tq), F32)) for _ in heads)
    states = _causal_flash_t((qi * tq) // tk, logit_fn, mask_fn, vt1_fn, [cqx for _, _, cqx in heads], init)
    ot = jnp.concatenate([_normalised_t(acc, hd) for _, acc in states], axis=0)
    o_ref[0] = ot.T.astype(o_ref.dtype)


def _fox_prompt_attention(q, ka, kb, vt, ct, *, batch, seq_len):
    n = batch * seq_len
    tq = _tile(seq_len, 512)
    tk = _tile(seq_len, 512)
    assert tk % tq == 0 and FOX_GROUP == 4
    nq = seq_len // tq
    gw = FOX_GROUP * FOX_HEAD_DIM
    q3 = q.reshape(batch, seq_len, FOX_NQ)
    ka4 = ka.reshape(FOX_KV_HEADS, batch, seq_len, LANES)
    kb4 = kb.reshape(FOX_KV_HEADS, batch, seq_len, LANES)
    ct3 = ct.reshape(FOX_KV_HEADS, FOX_GROUP, n)
    kspec = pl.BlockSpec((1, 1, seq_len, LANES), lambda b, h, i: (h, b, 0, 0))
    o = pl.pallas_call(
        functools.partial(_fox_attn_kernel, tq=tq, tk=tk),
        grid=(batch, FOX_KV_HEADS, nq),
        in_specs=[
            pl.BlockSpec((1, tq, gw), lambda b, h, i: (b, i, h)),
            kspec, kspec,
            pl.BlockSpec((1, FOX_HEAD_DIM + ONES_ROWS, seq_len), lambda b, h, i: (h, 0, b)),
            pl.BlockSpec((1, FOX_GROUP, tq), lambda b, h, i: (h, 0, b * nq + i)),
        ],
        out_specs=pl.BlockSpec((1, tq, gw), lambda b, h, i: (b, i, h)),
        out_shape=jax.ShapeDtypeStruct((batch, seq_len, FOX_NQ), BF16),
        compiler_params=_params(("parallel", "parallel", "arbitrary")),
        name="fox_prompt_attn",
    )(q3, ka4, kb4, vt, ct3)
    return o.reshape(n, FOX_NQ)


def _page_copies(pt_ref, b, chunk, slot, ppc, pairs):
    out = []
    for i in range(ppc):
        page = pt_ref[b, chunk * ppc + i]
        for hbm, buf, sem in pairs:
            out.append(pltpu.make_async_copy(hbm.at[page], buf.at[slot, i], sem.at[slot]))
    return out


def _paged_chunk_step(pt_ref, ci, nchunks, ppc, pairs, reverse):
    b, nb = pl.program_id(0), pl.num_programs(0)
    order = (lambda c: nchunks - 1 - c) if reverse else (lambda c: c)
    slot = (b * nchunks + ci) % 2
    last = ci + 1 == nchunks
    nxt_b = jnp.where(last, b + 1, b)
    nxt_chunk = order(jnp.where(last, 0, ci + 1))

    @pl.when(jnp.logical_and(b == 0, ci == 0))
    def _():
        for cp in _page_copies(pt_ref, b, order(ci), slot, ppc, pairs):
            cp.start()

    @pl.when(jnp.logical_or(ci + 1 < nchunks, b + 1 < nb))
    def _():
        for cp in _page_copies(pt_ref, nxt_b, nxt_chunk, 1 - slot, ppc, pairs):
            cp.start()

    for cp in _page_copies(pt_ref, b, order(ci), slot, ppc, pairs):
        cp.wait()
    return order(ci), slot


def _pad_rows(x, rows):
    return jnp.concatenate([x, jnp.zeros((rows - x.shape[0], x.shape[1]), x.dtype)], axis=0)


def _suffix_sum_lanes(x):
    lane = lax.broadcasted_iota(jnp.int32, x.shape, 1)
    y = x
    sh = 1
    while sh < LANES:
        y = y + jnp.where(lane < LANES - sh, pltpu.roll(y, LANES - sh, axis=1), 0.0)
        sh *= 2
    return y


def _fox_sample_kernel(pt_ref, q_ref, kn_ref, vn_ref, dcol_ref, dtp_ref, kt_hbm, vt_hbm, lf_hbm,
                       o_ref, kbuf, vbuf, lbuf, ksem, vsem, lsem, *, ppc, nchunks, t_new):
    nk = ppc * PAGE_SIZE
    hd, g, heads = FOX_HEAD_DIM, FOX_GROUP, FOX_HEADS
    rows = heads * t_new
    pairs = ((kt_hbm, kbuf, ksem), (vt_hbm, vbuf, vsem), (lf_hbm, lbuf, lsem))

    q = q_ref[...]
    lane = lax.broadcasted_iota(jnp.int32, (t_new, FOX_NKV), 1)
    qrows = []
    for h in range(heads):
        kh, gi = divmod(h, g)
        grp = q[:, kh * FOX_NKV:(kh + 1) * FOX_NKV]
        shift = ((kh - gi) * hd) % FOX_NKV
        r = pltpu.roll(grp, shift, axis=1) if shift else grp
        qrows.append(jnp.where((lane >= kh * hd) & (lane < (kh + 1) * hd), r, 0.0))
    qbd = jnp.concatenate(qrows, axis=0).astype(BF16)
    dcol3 = dcol_ref[0].reshape(heads, t_new, 1)

    def body(ci, carry):
        m, l, acc, run = carry
        _, slot = _paged_chunk_step(pt_ref, ci, nchunks, ppc, pairs, reverse=True)
        s_pages = [None] * ppc
        for i in reversed(range(ppc)):
            lf = lbuf[slot, i]
            y = _suffix_sum_lanes(lf)
            bias = y - lf + run
            run = run + y[:, 0:1]
            s = _dot(qbd, kbuf[slot, i].astype(BF16))
            s_pages[i] = s.reshape(heads, t_new, PAGE_SIZE) + dcol3 + bias[:, None, :]
        pps = ppc // PAGE_GROUPS
        local = []
        for u in range(PAGE_GROUPS):
            mu, lu, pu = _local_softmax(jnp.concatenate(s_pages[u * pps:(u + 1) * pps], axis=2))
            local.append((mu, lu, pu.reshape(rows, pps * PAGE_SIZE).astype(BF16)))
        parts = []
        for u, (mu, lu, pu) in enumerate(local):
            ou = jnp.zeros((rows, FOX_NKV), F32)
            for i in range(pps):
                ou = ou + _dot_nt(pu[:, i * PAGE_SIZE:(i + 1) * PAGE_SIZE], vbuf[slot, u * pps + i].astype(BF16))
            parts.append((mu, lu, ou))
        return _merge_softmax(m, l, acc, parts) + (run,)

    init = (jnp.full((heads, t_new, 1), MASKED, F32), jnp.zeros((heads, t_new, 1), F32),
            jnp.zeros((rows, FOX_NKV), F32), jnp.zeros((heads, 1), F32))
    m, l, acc, _ = lax.fori_loop(0, nchunks, body, init)

    kn = _pad_rows(kn_ref[...], LANES).astype(BF16)
    vn = _pad_rows(vn_ref[...], LANES).astype(BF16)
    s3 = _dot_nt(qbd, kn).reshape(heads, t_new, LANES) + dcol3 - dtp_ref[0][:, None, :]
    trow = lax.broadcasted_iota(jnp.int32, (t_new, LANES), 0)
    tcol = lax.broadcasted_iota(jnp.int32, (t_new, LANES), 1)
    s3 = jnp.where((tcol <= trow)[None], s3, MASKED)
    m, l, acc = _online_softmax_step(s3, vn, m, l, acc)
    o = acc / l.reshape(rows, 1)

    groups = []
    for kh in range(FOX_KV_HEADS):
        tot = jnp.zeros((t_new, FOX_NKV), F32)
        for gi in range(g):
            h = kh * g + gi
            blk = o[h * t_new:(h + 1) * t_new]
            shift = ((gi - kh) * hd) % FOX_NKV
            r = pltpu.roll(blk, shift, axis=1) if shift else blk
            tot = tot + jnp.where((lane >= gi * hd) & (lane < (gi + 1) * hd), r, 0.0)
        groups.append(tot)
    o_ref[...] = jnp.concatenate(groups, axis=1)


def _fox_sample_attention(q, k_new, v_new, ct, page_table, kt_pages, vt_pages, lf_pages, *, t_new):
    ns = q.shape[0]
    bd, n_pages = page_table.shape
    assert bd * t_new == ns and t_new == SUBLANES
    ppc = _tile(n_pages, 32)
    nchunks = n_pages // ppc
    rows = FOX_HEADS * t_new
    d_bht = ct.reshape(FOX_HEADS, bd, t_new).transpose(1, 0, 2)
    dcol = d_bht.reshape(bd, rows, 1)
    dtp = jnp.pad(d_bht, ((0, 0), (0, 0), (0, LANES - t_new)))
    anyspec = pl.BlockSpec(memory_space=pl.ANY)
    return pl.pallas_call(
        functools.partial(_fox_sample_kernel, ppc=ppc, nchunks=nchunks, t_new=t_new),
        grid_spec=pltpu.PrefetchScalarGridSpec(
            num_scalar_prefetch=1,
            grid=(bd,),
            in_specs=[
                pl.BlockSpec((t_new, FOX_NQ), lambda b, pt: (b, 0)),
                pl.BlockSpec((t_new, FOX_NKV), lambda b, pt: (b, 0)),
                pl.BlockSpec((t_new, FOX_NKV), lambda b, pt: (b, 0)),
                pl.BlockSpec((1, rows, 1), lambda b, pt: (b, 0, 0)),
                pl.BlockSpec((1, FOX_HEADS, LANES), lambda b, pt: (b, 0, 0)),
                anyspec, anyspec, anyspec,
            ],
            out_specs=pl.BlockSpec((t_new, FOX_NQ), lambda b, pt: (b, 0)),
            scratch_shapes=[pltpu.VMEM((2, ppc, FOX_NKV, PAGE_SIZE), F32),
                            pltpu.VMEM((2, ppc, FOX_NKV, PAGE_SIZE), F32),
                            pltpu.VMEM((2, ppc, FOX_HEADS, PAGE_SIZE), F32),
                            pltpu.SemaphoreType.DMA((2,)),
                            pltpu.SemaphoreType.DMA((2,)),
                            pltpu.SemaphoreType.DMA((2,))],
        ),
        out_shape=jax.ShapeDtypeStruct((ns, FOX_NQ), F32),
        compiler_params=_params(("arbitrary",)),
        name="fox_sample_attn",
    )(page_table, q, k_new, v_new, dcol, dtp, kt_pages, vt_pages, lf_pages)


def _oproj_kernel(o_ref, wo_ref, x_ref, g_ref, out_ref):
    y = _dot(o_ref[...].astype(BF16), wo_ref[...])
    out_ref[...] = x_ref[...] + _rms(y, g_ref[...])


def _out_project(o, w_o, x, g):
    n, d = x.shape
    e = o.shape[1]
    tm = _tile(n, 512)
    row = lambda i: (i, 0)
    const = lambda i: (0, 0)
    return pl.pallas_call(
        _oproj_kernel,
        grid=(n // tm,),
        in_specs=[pl.BlockSpec((tm, e), row), pl.BlockSpec((e, d), const),
                  pl.BlockSpec((tm, d), row), pl.BlockSpec((1, d), const)],
        out_specs=pl.BlockSpec((tm, d), row),
        out_shape=jax.ShapeDtypeStruct((n, d), F32),
        compiler_params=_params(("parallel",)),
        name="out_proj",
    )(o, w_o.astype(BF16), x, g.reshape(1, d))


def _mla_oproj_kernel(ol_ref, wuv_ref, wo_ref, x_ref, g_ref, out_ref):
    parts = []
    for j in range(MLA_HEADS // 2):
        pair = jnp.concatenate([ol_ref[2 * j], ol_ref[2 * j + 1]], axis=1).astype(BF16)
        parts.append(_dot(pair, wuv_ref[j]).astype(BF16))
    y = _dot(jnp.concatenate(parts, axis=1), wo_ref[...])
    out_ref[...] = x_ref[...] + _rms(y, g_ref[...])


def _mla_out_project(o_lat, w_uv, w_o, x, g):
    n, d = x.shape
    tm = _tile(n, 512)
    npair = MLA_HEADS // 2
    wp = w_uv.reshape(MLA_KV_LORA, npair, 2, MLA_V_DIM).transpose(1, 2, 0, 3)
    z = jnp.zeros_like(wp[:, 0])
    wbd = jnp.concatenate([jnp.concatenate([wp[:, 0], z], axis=2),
                           jnp.concatenate([z, wp[:, 1]], axis=2)], axis=1).astype(BF16)
    row = lambda i: (i, 0)
    const = lambda i: (0, 0)
    return pl.pallas_call(
        _mla_oproj_kernel,
        grid=(n // tm,),
        in_specs=[pl.BlockSpec((MLA_HEADS, tm, MLA_KV_LORA), lambda i: (0, i, 0)),
                  pl.BlockSpec(wbd.shape, lambda i: (0, 0, 0)),
                  pl.BlockSpec(w_o.shape, const),
                  pl.BlockSpec((tm, d), row), pl.BlockSpec((1, d), const)],
        out_specs=pl.BlockSpec((tm, d), row),
        out_shape=jax.ShapeDtypeStruct((n, d), F32),
        compiler_params=_params(("parallel",)),
        name="mla_out_proj",
    )(o_lat, wbd, w_o.astype(BF16), x, g.reshape(1, d))


def _mla_proj_kernel(x_ref, g_ref, wa_ref, qn_ref, kvn_ref, wnope_ref, wpe_ref, wper_ref, wuk_ref,
                     cos_ref, sin_ref, qp_ref, kvp_ref, ckv_ref, ckvt_ref, kpe_ref, *, logit_scale):
    tm = x_ref.shape[0]
    scale = (MLA_NOPE + MLA_ROPE) ** -0.5 * logit_scale
    h = _rms(x_ref[...], g_ref[...]).astype(BF16)
    a = _dot(h, wa_ref[...])
    cos, sin = cos_ref[...], sin_ref[...]
    c_q = _rms(a[:, :MLA_Q_LORA], qn_ref[...]).astype(BF16)
    c_kv = _rms(a[:, MLA_Q_LORA:MLA_Q_LORA + MLA_KV_LORA], kvn_ref[...])
    r0 = MLA_Q_LORA + MLA_KV_LORA
    kpe_rep = a[:, r0:r0 + LANES] * cos + a[:, r0 + LANES:r0 + 2 * LANES] * sin
    ckv_ref[...] = c_kv
    ckvt_ref[0:MLA_KV_LORA, :] = c_kv.T.astype(BF16)
    ckvt_ref[MLA_KV_LORA:MLA_KV_LORA + ONES_ROWS, :] = jnp.ones((ONES_ROWS, tm), BF16)
    kpe_ref[...] = kpe_rep[:, :MLA_ROPE]
    kvp_ref[...] = jnp.concatenate([c_kv, kpe_rep], axis=1).astype(BF16)

    nrep = MLA_HEADS // ROPE_REP
    cos_all = jnp.concatenate([cos] * nrep, axis=1)
    sin_all = jnp.concatenate([sin] * nrep, axis=1)
    qpe = (_dot(c_q, wpe_ref[...]) * cos_all + _dot(c_q, wper_ref[...]) * sin_all) * scale
    qnope = _dot(c_q, wnope_ref[...]).astype(BF16)
    lane = lax.broadcasted_iota(jnp.int32, (tm, LANES), 1)
    for j in range(MLA_HEADS // 2):
        qlat2 = _dot(qnope[:, j * LANES:(j + 1) * LANES], wuk_ref[j]) * scale
        for w in range(2):
            hh = 2 * j + w
            grp = qpe[:, (hh // ROPE_REP) * LANES:(hh // ROPE_REP + 1) * LANES]
            quarter = hh % ROPE_REP
            pe = jnp.where((lane >= quarter * MLA_ROPE) & (lane < (quarter + 1) * MLA_ROPE), grp, 0.0)
            qp_ref[hh] = jnp.concatenate([qlat2[:, w * LANES:(w + 1) * LANES], pe], axis=1).astype(qp_ref.dtype)


def _rot_half_cols(w):
    half = MLA_ROPE // 2
    return jnp.concatenate([-w[..., half:], w[..., :half]], axis=-1)


def _mla_project(x, g, w_a, q_norm, kv_norm, w_uq, w_uk, pos, *, q_dtype, logit_scale):
    n, d = x.shape
    tm = _tile(n, 256)
    npos = pos.shape[0]
    assert n % npos == 0 and npos % tm == 0
    grid = (n // npos, npos // tm)
    g1 = grid[1]

    half = MLA_ROPE // 2
    inv = ROPE_THETA ** (-jnp.arange(half, dtype=F32) / half)
    ang = pos.astype(F32)[:, None] * inv[None, :]
    cos = jnp.tile(jnp.cos(ang), (1, 2 * ROPE_REP))
    sin = jnp.tile(jnp.sin(ang), (1, 2 * ROPE_REP))

    r0 = MLA_Q_LORA + MLA_KV_LORA
    w_pe = w_a[:, r0:]
    wa_ext = jnp.concatenate([w_a[:, :r0], jnp.tile(w_pe, (1, ROPE_REP)),
                              jnp.tile(_rot_half_cols(w_pe), (1, ROPE_REP))], axis=1).astype(BF16)
    w_nope = w_uq[:, :, :MLA_NOPE].reshape(MLA_Q_LORA, MLA_HEADS * MLA_NOPE).astype(BF16)
    w_qpe = w_uq[:, :, MLA_NOPE:]
    w_pe_q = w_qpe.reshape(MLA_Q_LORA, MLA_HEADS * MLA_ROPE).astype(BF16)
    w_pe_qr = _rot_half_cols(w_qpe).reshape(MLA_Q_LORA, MLA_HEADS * MLA_ROPE).astype(BF16)
    npair = MLA_HEADS // 2
    wk = w_uk.reshape(MLA_KV_LORA, npair, 2, MLA_NOPE).transpose(1, 2, 3, 0)
    z = jnp.zeros_like(wk[:, 0])
    wuk_bd = jnp.concatenate([jnp.concatenate([wk[:, 0], z], axis=2),
                              jnp.concatenate([z, wk[:, 1]], axis=2)], axis=1).astype(BF16)

    tok = lambda i, j: (i * g1 + j, 0)
    const = lambda i, j: (0, 0)
    posb = lambda i, j: (j, 0)
    return pl.pallas_call(
        functools.partial(_mla_proj_kernel, logit_scale=logit_scale),
        grid=grid,
        in_specs=[
            pl.BlockSpec((tm, d), tok),
            pl.BlockSpec((1, d), const),
            pl.BlockSpec(wa_ext.shape, const),
            pl.BlockSpec((1, MLA_Q_LORA), const),
            pl.BlockSpec((1, MLA_KV_LORA), const),
            pl.BlockSpec(w_nope.shape, const),
            pl.BlockSpec(w_pe_q.shape, const),
            pl.BlockSpec(w_pe_qr.shape, const),
            pl.BlockSpec(wuk_bd.shape, lambda i, j: (0, 0, 0)),
            pl.BlockSpec((tm, LANES), posb),
            pl.BlockSpec((tm, LANES), posb),
        ],
        out_specs=[
            pl.BlockSpec((MLA_HEADS, tm, MLA_QK), lambda i, j: (0, i * g1 + j, 0)),
            pl.BlockSpec((tm, MLA_QK), tok),
            pl.BlockSpec((tm, MLA_KV_LORA), tok),
            pl.BlockSpec((MLA_KV_LORA + ONES_ROWS, tm), lambda i, j: (0, i * g1 + j)),
            pl.BlockSpec((tm, MLA_ROPE), tok),
        ],
        out_shape=[
            jax.ShapeDtypeStruct((MLA_HEADS, n, MLA_QK), q_dtype),
            jax.ShapeDtypeStruct((n, MLA_QK), BF16),
            jax.ShapeDtypeStruct((n, MLA_KV_LORA), F32),
            jax.ShapeDtypeStruct((MLA_KV_LORA + ONES_ROWS, n), BF16),
            jax.ShapeDtypeStruct((n, MLA_ROPE), F32),
        ],
        compiler_params=_params(("parallel", "parallel")),
        name="mla_proj",
    )(x, g.reshape(1, d), wa_ext, q_norm.reshape(1, -1), kv_norm.reshape(1, -1),
      w_nope, w_pe_q, w_pe_qr, wuk_bd, cos, sin)


def _mla_attn_kernel(q_ref, kv_ref, ckvt_ref, o_ref, *, tq, tk):
    qi = pl.program_id(1)
    nh = MLA_HEADS
    ngroups = 4
    cols = nh * tq // ngroups
    q = q_ref[...].reshape(nh * tq, MLA_QK)
    qs = [q[i * cols:(i + 1) * cols] for i in range(ngroups)]
    tile = lambda j: pl.ds(pl.multiple_of(j * tk, tk), tk)
    logit_fn = lambda j, c: _dot_nt(kv_ref[0, tile(j), :], qs[c])
    mask_fn = lambda st, j: _causal_mask_t(st, j * tk, qi * tq, tq)
    vt1_fn = lambda j: ckvt_ref[:, tile(j)]
    init = tuple((jnp.full((1, cols), MASKED, F32), jnp.zeros((MLA_KV_LORA + ONES_ROWS, cols), F32))
                 for _ in range(ngroups))
    states = _causal_flash_t((qi * tq) // tk, logit_fn, mask_fn, vt1_fn, [None] * ngroups, init)
    ot = jnp.concatenate([_normalised_t(acc, MLA_KV_LORA) for _, acc in states], axis=1)
    o_ref[...] = ot.T.reshape(nh, tq, MLA_KV_LORA).astype(o_ref.dtype)


def _mla_prompt_attention(qp, kvp, ckvt, *, batch, seq_len):
    n = batch * seq_len
    tq = _tile(seq_len, 128)
    tk = _tile(seq_len, 512)
    assert tk % tq == 0
    nq = seq_len // tq
    kv3 = kvp.reshape(batch, seq_len, MLA_QK)
    return pl.pallas_call(
        functools.partial(_mla_attn_kernel, tq=tq, tk=tk),
        grid=(batch, nq),
        in_specs=[pl.BlockSpec((MLA_HEADS, tq, MLA_QK), lambda b, i: (0, b * nq + i, 0)),
                  pl.BlockSpec((1, seq_len, MLA_QK), lambda b, i: (b, 0, 0)),
                  pl.BlockSpec((MLA_KV_LORA + ONES_ROWS, seq_len), lambda b, i: (0, b))],
        out_specs=pl.BlockSpec((MLA_HEADS, tq, MLA_KV_LORA), lambda b, i: (0, b * nq + i, 0)),
        out_shape=jax.ShapeDtypeStruct((MLA_HEADS, n, MLA_KV_LORA), BF16),
        compiler_params=_params(("parallel", "arbitrary")),
        name="mla_prompt_attn",
    )(qp, kv3, ckvt)


def _mla_sample_kernel(pt_ref, q_ref, cn_ref, rn_ref, c_hbm, r_hbm, o_ref, cbuf, rbuf, csem, rsem,
                       *, ppc, nchunks, t_new):
    nk = ppc * PAGE_SIZE
    nh = MLA_HEADS
    rows = nh * t_new
    pairs = ((c_hbm, cbuf, csem), (r_hbm, rbuf, rsem))

    q = q_ref[...].reshape(rows, MLA_QK)
    qlat = q[:, :MLA_KV_LORA].astype(BF16)
    grp = q[:, MLA_KV_LORA:]
    qpe = grp[:, :MLA_ROPE]
    for i in range(1, ROPE_REP):
        qpe = qpe + grp[:, i * MLA_ROPE:(i + 1) * MLA_ROPE]
    qpe = qpe.astype(BF16)

    def body(c, carry):
        _, slot = _paged_chunk_step(pt_ref, c, nchunks, ppc, pairs, reverse=False)
        pps = ppc // PAGE_GROUPS
        nsub = pps * PAGE_SIZE
        logits = []
        for u in range(PAGE_GROUPS):
            ckv = cbuf[slot, u * pps:(u + 1) * pps].reshape(nsub, MLA_KV_LORA).astype(BF16)
            s_pe = jnp.concatenate([_dot(qpe, rbuf[slot, u * pps + i].astype(BF16)) for i in range(pps)], axis=1)
            logits.append((ckv, (_dot_nt(qlat, ckv) + s_pe).reshape(nh, t_new, nsub)))
        local = [(ckv,) + _local_softmax(s3) for ckv, s3 in logits]
        parts = [(mu, lu, _dot(pu.reshape(rows, nsub).astype(BF16), ckv)) for ckv, mu, lu, pu in local]
        return _merge_softmax(*carry, parts)

    init = (jnp.full((nh, t_new, 1), MASKED, F32), jnp.zeros((nh, t_new, 1), F32),
            jnp.zeros((rows, MLA_KV_LORA), F32))
    m, l, acc = lax.fori_loop(0, nchunks, body, init)

    cn = _pad_rows(cn_ref[...], LANES).astype(BF16)
    rn = _pad_rows(rn_ref[...], LANES).astype(BF16)
    s3 = (_dot_nt(qlat, cn) + _dot_nt(qpe, rn)).reshape(nh, t_new, LANES)
    trow = lax.broadcasted_iota(jnp.int32, (t_new, LANES), 0)
    tcol = lax.broadcasted_iota(jnp.int32, (t_new, LANES), 1)
    s3 = jnp.where((tcol <= trow)[None], s3, MASKED)
    m, l, acc = _online_softmax_step(s3, cn, m, l, acc)
    o_ref[...] = (acc / l.reshape(rows, 1)).reshape(nh, t_new, MLA_KV_LORA)


def _mla_sample_attention(qp, c_new, kpe_new, page_table, ckv_pages, kpet_pages, *, t_new):
    ns = c_new.shape[0]
    bd, n_pages = page_table.shape
    assert bd * t_new == ns and t_new == SUBLANES
    ppc = _tile(n_pages, 32)
    nchunks = n_pages // ppc
    return pl.pallas_call(
        functools.partial(_mla_sample_kernel, ppc=ppc, nchunks=nchunks, t_new=t_new),
        grid_spec=pltpu.PrefetchScalarGridSpec(
            num_scalar_prefetch=1,
            grid=(bd,),
            in_specs=[
                pl.BlockSpec((MLA_HEADS, t_new, MLA_QK), lambda b, pt: (0, b, 0)),
                pl.BlockSpec((t_new, MLA_KV_LORA), lambda b, pt: (b, 0)),
                pl.BlockSpec((t_new, MLA_ROPE), lambda b, pt: (b, 0)),
                pl.BlockSpec(memory_space=pl.ANY),
                pl.BlockSpec(memory_space=pl.ANY),
            ],
            out_specs=pl.BlockSpec((MLA_HEADS, t_new, MLA_KV_LORA), lambda b, pt: (0, b, 0)),
            scratch_shapes=[pltpu.VMEM((2, ppc, PAGE_SIZE, MLA_KV_LORA), F32),
                            pltpu.VMEM((2, ppc, MLA_ROPE, PAGE_SIZE), F32),
                            pltpu.SemaphoreType.DMA((2,)),
                            pltpu.SemaphoreType.DMA((2,))],
        ),
        out_shape=jax.ShapeDtypeStruct((MLA_HEADS, ns, MLA_KV_LORA), F32),
        compiler_params=_params(("arbitrary",)),
        name="mla_sample_attn",
    )(page_table, qp, c_new, kpe_new, ckv_pages, kpet_pages)


def _ffn_kernel(x_ref, g1_ref, g2_ref, win_ref, cw_ref, cb_ref, wout_ref, *rest, fc, seq_rows, chained):
    if chained:
        out_ref, st_out_ref, act_ref, carry_ref = rest
    else:
        st_ref, out_ref, st_out_ref, act_ref = rest
    tm, d = x_ref.shape
    dff = wout_ref.shape[0]
    x = x_ref[...]
    h = _rms(x, g1_ref[...]).astype(BF16)

    if chained:
        @pl.when(pl.program_id(1) == 0)
        def _():
            carry_ref[...] = jnp.zeros_like(carry_ref)
        row = lax.broadcasted_iota(jnp.int32, (tm, fc), 0)
    else:
        nseq = tm // seq_rows
        row = lax.broadcasted_iota(jnp.int32, (nseq, seq_rows, fc), 1)

    for f in range(dff // fc):
        cs = slice(f * fc, (f + 1) * fc)
        gate = _dot(h, win_ref[:, cs])
        up = _dot(h, win_ref[:, dff + f * fc:dff + (f + 1) * fc])
        w0, w1, w2 = cw_ref[0:1, cs], cw_ref[1:2, cs], cw_ref[2:3, cs]
        if chained:
            p0, p1 = carry_ref[0:1, cs], carry_ref[1:2, cs]
            sh1 = jnp.where(row == 0, p1, pltpu.roll(gate, 1, axis=0))
            sh2 = jnp.where(row == 0, p0, jnp.where(row == 1, p1, pltpu.roll(gate, 2, axis=0)))
            carry_ref[0:2, cs] = gate[tm - 2:tm]
            st_out_ref[0, :, cs] = gate[tm - 2:tm]
            conv = cb_ref[:, cs] + w0 * sh2 + w1 * sh1 + w2 * gate
        else:
            g3 = gate.reshape(nseq, seq_rows, fc)
            p0, p1 = st_ref[:, 0:1, cs], st_ref[:, 1:2, cs]
            sh1 = jnp.where(row == 0, p1, pltpu.roll(g3, 1, axis=1))
            sh2 = jnp.where(row == 0, p0, jnp.where(row == 1, p1, pltpu.roll(g3, 2, axis=1)))
            st_out_ref[:, :, cs] = g3[:, seq_rows - 2:seq_rows, :]
            conv = (cb_ref[:, cs] + w0 * sh2 + w1 * sh1 + w2 * g3).reshape(tm, fc)
        act_ref[:, cs] = (jax.nn.gelu(conv, approximate=True) * up).astype(BF16)

    y = _dot(act_ref[...], wout_ref[...])
    out_ref[...] = x + _rms(y, g2_ref[...])


def _conv_ffn_block(x, g1, g2, w_in, conv_w, conv_b, w_out, state, *, seq_len):
    n, d = x.shape
    dff = w_out.shape[0]
    fc = 256
    assert dff % fc == 0 and CONV_W == 3
    nseq_total = n // seq_len
    chained = state is None
    wspec = dict(pipeline_mode=pl.Buffered(1))
    if chained:
        tm = _tile(seq_len, 512)
        grid = (nseq_total, seq_len // tm)
        g1n = grid[1]
        tok = lambda i, j: (i * g1n + j, 0)
        const = lambda i, j: (0, 0)
        st_spec = pl.BlockSpec((1, CONV_W - 1, dff), lambda i, j: (i, 0, 0))
        extra_in, extra_specs = [], []
        scratch = [pltpu.VMEM((tm, dff), BF16), pltpu.VMEM((SUBLANES, dff), F32)]
        sem = ("arbitrary", "arbitrary")
    else:
        sb = _tile(nseq_total, 32)
        tm = sb * seq_len
        grid = (nseq_total // sb,)
        tok = lambda i: (i, 0)
        const = lambda i: (0, 0)
        st_spec = pl.BlockSpec((sb, CONV_W - 1, dff), lambda i: (i, 0, 0))
        extra_in, extra_specs = [state], [st_spec]
        scratch = [pltpu.VMEM((tm, dff), BF16)]
        sem = ("arbitrary",)
    out, st_new = pl.pallas_call(
        functools.partial(_ffn_kernel, fc=fc, seq_rows=seq_len, chained=chained),
        grid=grid,
        in_specs=[
            pl.BlockSpec((tm, d), tok),
            pl.BlockSpec((1, d), const),
            pl.BlockSpec((1, d), const),
            pl.BlockSpec((d, 2 * dff), const, **wspec),
            pl.BlockSpec((CONV_W, dff), const),
            pl.BlockSpec((1, dff), const),
            pl.BlockSpec((dff, d), const, **wspec),
        ] + extra_specs,
        out_specs=[pl.BlockSpec((tm, d), tok), st_spec],
        out_shape=[jax.ShapeDtypeStruct((n, d), F32),
                   jax.ShapeDtypeStruct((nseq_total, CONV_W - 1, dff), F32)],
        scratch_shapes=scratch,
        compiler_params=_params(sem),
        name="conv_ffn",
    )(x, g1.reshape(1, d), g2.reshape(1, d), w_in.astype(BF16), conv_w, conv_b.reshape(1, dff),
      w_out.astype(BF16), *extra_in)
    return out, st_new


def kernel(x_prompt, x_sample, cache_fox_k, cache_fox_v, cache_fox_logf, cache_mla_ckv, cache_mla_kpe, state_conv, page_table, norm_mix_pre, norm_mix_post, norm_ffn_pre, norm_ffn_post, fox_w_in, fox_b_f, fox_w_o, mla_w_a, mla_q_norm, mla_kv_norm, mla_w_uq, mla_w_uk, mla_w_uv, mla_w_o, ffn_w_in, ffn_conv_w, ffn_conv_b, ffn_w_out):
    batch, seq_len, d = x_prompt.shape
    bd, t_new, _ = x_sample.shape
    depth = norm_mix_pre.shape[0]
    n_pool = cache_fox_k.shape[1]
    past = page_table.shape[1] * PAGE_SIZE
    xp = x_prompt.reshape(batch * seq_len, d)
    xs = x_sample.reshape(bd * t_new, d)

    fox_kt_pages = jnp.transpose(cache_fox_k, (0, 1, 3, 4, 2)).reshape(-1, FOX_NKV, PAGE_SIZE)
    fox_vt_pages = jnp.transpose(cache_fox_v, (0, 1, 3, 4, 2)).reshape(-1, FOX_NKV, PAGE_SIZE)
    fox_f_pages = jnp.transpose(cache_fox_logf, (0, 1, 3, 2)).reshape(-1, FOX_HEADS, PAGE_SIZE)
    mla_c_pages = cache_mla_ckv.reshape(-1, PAGE_SIZE, MLA_KV_LORA)
    mla_rt_pages = jnp.transpose(cache_mla_kpe, (0, 1, 3, 2)).reshape(-1, MLA_ROPE, PAGE_SIZE)

    outs_p = {k: [] for k in ("fk", "fv", "ff", "mc", "mr", "cv")}
    outs_s = {k: [] for k in ("fk", "fv", "ff", "mc", "mr", "cv")}
    for i in range(depth):
        li = i // N_MIXERS
        pages = page_table + li * n_pool
        if i % N_MIXERS == 0:
            w_in, b_f, w_o = fox_w_in[li], fox_b_f[li], fox_w_o[li]
            q, _, _, ktf, vtf, ka, kb, vt, logf, ct = _fox_project(xp, norm_mix_pre[i], w_in, b_f, seq_len=seq_len,
                                                                   q_dtype=BF16, logit_scale=LOG2E)
            o = _fox_prompt_attention(q, ka, kb, vt, ct, batch=batch, seq_len=seq_len)
            xp = _out_project(o, w_o, xp, norm_mix_post[i])
            heads_t = lambda a: a.reshape(batch, FOX_KV_HEADS, FOX_HEAD_DIM, seq_len).transpose(0, 3, 1, 2)
            outs_p["fk"].append(heads_t(ktf))
            outs_p["fv"].append(heads_t(vtf))
            outs_p["ff"].append(logf.reshape(batch, seq_len, FOX_HEADS))

            q, k, v, _, _, _, _, _, logf, ct = _fox_project(xs, norm_mix_pre[i], w_in, b_f, seq_len=t_new,
                                                            q_dtype=F32, logit_scale=1.0)
            o = _fox_sample_attention(q, k, v, ct, pages, fox_kt_pages, fox_vt_pages, fox_f_pages, t_new=t_new)
            xs = _out_project(o, w_o, xs, norm_mix_post[i])
            outs_s["fk"].append(k.reshape(bd, t_new, FOX_KV_HEADS, FOX_HEAD_DIM))
            outs_s["fv"].append(v.reshape(bd, t_new, FOX_KV_HEADS, FOX_HEAD_DIM))
            outs_s["ff"].append(logf.reshape(bd, t_new, FOX_HEADS))
        else:
            args = (mla_w_a[li], mla_q_norm[li], mla_kv_norm[li], mla_w_uq[li], mla_w_uk[li])
            qp, kvp, ckv, ckvt, kpe = _mla_project(xp, norm_mix_pre[i], *args, jnp.arange(seq_len),
                                                   q_dtype=BF16, logit_scale=LOG2E)
            o_lat = _mla_prompt_attention(qp, kvp, ckvt, batch=batch, seq_len=seq_len)
            xp = _mla_out_project(o_lat, mla_w_uv[li], mla_w_o[li], xp, norm_mix_post[i])
            outs_p["mc"].append(ckv.reshape(batch, seq_len, MLA_KV_LORA))
            outs_p["mr"].append(kpe.reshape(batch, seq_len, MLA_ROPE))

            pos_s = jnp.tile(past + jnp.arange(t_new), bd)
            qp, _, ckv, _, kpe = _mla_project(xs, norm_mix_pre[i], *args, pos_s, q_dtype=F32, logit_scale=1.0)
            o_lat = _mla_sample_attention(qp, ckv, kpe, pages, mla_c_pages, mla_rt_pages, t_new=t_new)
            xs = _mla_out_project(o_lat, mla_w_uv[li], mla_w_o[li], xs, norm_mix_post[i])
            outs_s["mc"].append(ckv.reshape(bd, t_new, MLA_KV_LORA))
            outs_s["mr"].append(kpe.reshape(bd, t_new, MLA_ROPE))

        ffn = (norm_ffn_pre[i], norm_ffn_post[i], ffn_w_in[i], ffn_conv_w[i], ffn_conv_b[i], ffn_w_out[i])
        xp, cvp = _conv_ffn_block(xp, *ffn, None, seq_len=seq_len)
        xs, cvs = _conv_ffn_block(xs, *ffn, state_conv[i], seq_len=t_new)
        outs_p["cv"].append(cvp)
        outs_s["cv"].append(cvs)

    ldt = cache_fox_logf.dtype
    st = jnp.stack
    return (xp.reshape(batch, seq_len, d), xs.reshape(bd, t_new, d),
            st(outs_p["fk"]), st(outs_p["fv"]), st(outs_p["ff"]).astype(ldt), st(outs_p["mc"]), st(outs_p["mr"]), st(outs_p["cv"]),
            st(outs_s["fk"]), st(outs_s["fv"]), st(outs_s["ff"]).astype(ldt), st(outs_s["mc"]), st(outs_s["mr"]), st(outs_s["cv"]))
```

```python
import functools

import jax
import jax.numpy as jnp
from jax import lax
from jax.experimental import pallas as pl
from jax.experimental.pallas import tpu as pltpu

F32 = jnp.float32
BF16 = jnp.bfloat16

N_MIXERS = 2
FOX_HEADS = 16
FOX_KV_HEADS = 4
FOX_GROUP = FOX_HEADS // FOX_KV_HEADS
FOX_HEAD_DIM = 64
FOX_NQ = FOX_HEADS * FOX_HEAD_DIM
FOX_NKV = FOX_KV_HEADS * FOX_HEAD_DIM
MLA_HEADS = 16
MLA_Q_LORA = 256
MLA_KV_LORA = 128
MLA_NOPE = 64
MLA_ROPE = 32
MLA_V_DIM = 64
ROPE_THETA = 10000.0
CONV_W = 3
PAGE_SIZE = 128
EPS = 1e-6

LANES = 128
SUBLANES = 8
V7X_VMEM_LIMIT_BYTES = 56 * 1024 * 1024

MASKED = -1e30
MLA_QK = 2 * LANES
ROPE_REP = LANES // MLA_ROPE

LOG2E = 1.4426950408889634
ONES_ROWS = 16
PAGE_GROUPS = 4
PAGE_BUFFERS = 3
FOX_BIAS_PIECES = 3
FOX_BIAS_LANE0 = FOX_HEAD_DIM

NT_DIMS = (((1,), (1,)), ((), ()))


def _params(sem):
    return pltpu.CompilerParams(dimension_semantics=sem, vmem_limit_bytes=V7X_VMEM_LIMIT_BYTES)


def _tile(n, pref):
    t = min(n, pref)
    assert n % t == 0, (n, pref)
    return t


def _rms(x, g):
    return x * lax.rsqrt(jnp.mean(x * x, axis=-1, keepdims=True) + EPS) * g


def _log_sigmoid(x):
    return jnp.minimum(x, 0.0) - jnp.log1p(jnp.exp(-jnp.abs(x)))


def _split3(x):
    hi = x.astype(BF16)
    r1 = x - hi.astype(F32)
    mid = r1.astype(BF16)
    lo = (r1 - mid.astype(F32)).astype(BF16)
    return hi, mid, lo


def _dot(a, b):
    return jnp.dot(a, b, preferred_element_type=F32)


def _dot_nt(a, b):
    return lax.dot_general(a, b, NT_DIMS, preferred_element_type=F32)


def _fox_proj_kernel(x_ref, g_ref, w_ref, wft_ref, bf_ref, bft_ref, u_ref, l_ref, pa_ref, pb_ref,
                     q_ref, k_ref, v_ref, ktf_ref, vtf_ref, ka_ref, kb_ref, vt_ref, logf_ref, ct_ref,
                     carry_ref, carryt_ref, *, logit_scale):
    tm = x_ref.shape[0]
    hd = FOX_HEAD_DIM

    @pl.when(pl.program_id(1) == 0)
    def _():
        carry_ref[...] = jnp.zeros_like(carry_ref)
        carryt_ref[...] = jnp.zeros_like(carryt_ref)

    h = _rms(x_ref[...], g_ref[...]).astype(BF16)
    proj = _dot(h, w_ref[...])
    q_ref[...] = (proj[:, :FOX_NQ] * (hd ** -0.5 * logit_scale)).astype(q_ref.dtype)
    k = proj[:, FOX_NQ:FOX_NQ + FOX_NKV]
    v = proj[:, FOX_NQ + FOX_NKV:FOX_NQ + 2 * FOX_NKV]
    k_ref[...] = k
    v_ref[...] = v
    ktf_ref[0] = k.T
    vtf_ref[0] = v.T
    vt = vtf_ref[0].astype(BF16)
    for kh in range(FOX_KV_HEADS):
        vt_ref[kh, 0:hd, :] = vt[kh * hd:(kh + 1) * hd]
        vt_ref[kh, hd:hd + ONES_ROWS, :] = jnp.ones((ONES_ROWS, tm), BF16)
    f0 = FOX_NQ + 2 * FOX_NKV
    logf = _log_sigmoid(proj[:, f0:f0 + FOX_HEADS] + bf_ref[...])
    logf_ref[...] = logf
    lmat = l_ref[...]
    c = carry_ref[...]
    for piece in _split3(logf):
        c = c + _dot(lmat, piece)
    carry_ref[...] = c[tm - 1:tm, :]
    placed_a = jnp.zeros((tm, FOX_KV_HEADS * LANES), F32)
    placed_b = jnp.zeros((tm, FOX_KV_HEADS * LANES), F32)
    for p, piece in enumerate(_split3(c * logit_scale)):
        placed_a = placed_a + _dot(piece, pa_ref[p])
        placed_b = placed_b + _dot(piece, pb_ref[p])
    lane = lax.broadcasted_iota(jnp.int32, (tm, LANES), 1)
    for kh in range(FOX_KV_HEADS):
        grp = k[:, (kh // 2) * LANES:(kh // 2 + 1) * LANES]
        swapped = pltpu.roll(grp, hd, axis=1)
        low, high = (grp, swapped) if kh % 2 == 0 else (swapped, grp)
        sl = slice(kh * LANES, (kh + 1) * LANES)
        ka_ref[kh] = jnp.where(lane < hd, low, placed_a[:, sl]).astype(BF16)
        kb_ref[kh] = jnp.where(lane >= hd, high, placed_b[:, sl]).astype(BF16)
    logft = _log_sigmoid(_dot_nt(wft_ref[...], h) + bft_ref[...])
    umat = u_ref[...]
    ct = carryt_ref[...]
    for piece in _split3(logft):
        ct = ct + _dot(piece, umat)
    ct_ref[...] = ct
    carryt_ref[...] = ct[:, tm - 1:tm]


def _fox_project(x, g, w_in, b_f, *, seq_len, q_dtype, logit_scale):
    n, d = x.shape
    tm = _tile(n, 512)
    if seq_len >= tm:
        assert seq_len % tm == 0
        grid = (n // seq_len, seq_len // tm)
        grp = jnp.zeros((tm,), jnp.int32)
    else:
        assert tm % seq_len == 0
        grid = (n // tm, 1)
        grp = jnp.arange(tm, dtype=jnp.int32) // seq_len
    g1 = grid[1]
    pos = jnp.arange(tm, dtype=jnp.int32)
    same = grp[:, None] == grp[None, :]
    umat = (same & (pos[:, None] <= pos[None, :])).astype(BF16)
    lmat = umat.T
    f0 = FOX_NQ + 2 * FOX_NKV
    w_main = jnp.pad(w_in, ((0, 0), (0, LANES - FOX_HEADS))).astype(BF16)
    wft = w_in[:, f0:].T.astype(BF16)
    wcols = w_main.shape[1]
    hh = jnp.arange(FOX_HEADS)
    pp = jnp.arange(FOX_BIAS_PIECES)
    col_b = (hh // FOX_GROUP)[None, :] * LANES + FOX_BIAS_PIECES * (hh % FOX_GROUP)[None, :] + pp[:, None]
    cols = jnp.arange(FOX_KV_HEADS * LANES)
    place_b = (col_b[:, :, None] == cols[None, None, :]).astype(BF16)
    place_a = (col_b[:, :, None] + FOX_HEAD_DIM == cols[None, None, :]).astype(BF16)

    tok = lambda i, j: (i * g1 + j, 0)
    tok3 = lambda i, j: (0, i * g1 + j, 0)
    tokt = lambda i, j: (0, i * g1 + j)
    const = lambda i, j: (0, 0)
    const3 = lambda i, j: (0, 0, 0)
    outs = pl.pallas_call(
        functools.partial(_fox_proj_kernel, logit_scale=logit_scale),
        grid=grid,
        in_specs=[
            pl.BlockSpec((tm, d), tok),
            pl.BlockSpec((1, d), const),
            pl.BlockSpec((d, wcols), const),
            pl.BlockSpec((FOX_HEADS, d), const),
            pl.BlockSpec((1, FOX_HEADS), const),
            pl.BlockSpec((FOX_HEADS, 1), const),
            pl.BlockSpec((tm, tm), const),
            pl.BlockSpec((tm, tm), const),
            pl.BlockSpec(place_a.shape, const3),
            pl.BlockSpec(place_b.shape, const3),
        ],
        out_specs=[
            pl.BlockSpec((tm, FOX_NQ), tok),
            pl.BlockSpec((tm, FOX_NKV), tok),
            pl.BlockSpec((tm, FOX_NKV), tok),
            pl.BlockSpec((1, FOX_NKV, tm), lambda i, j: (i, 0, j)),
            pl.BlockSpec((1, FOX_NKV, tm), lambda i, j: (i, 0, j)),
            pl.BlockSpec((FOX_KV_HEADS, tm, LANES), tok3),
            pl.BlockSpec((FOX_KV_HEADS, tm, LANES), tok3),
            pl.BlockSpec((FOX_KV_HEADS, FOX_HEAD_DIM + ONES_ROWS, tm), lambda i, j: (0, 0, i * g1 + j)),
            pl.BlockSpec((tm, FOX_HEADS), tok),
            pl.BlockSpec((FOX_HEADS, tm), tokt),
        ],
        out_shape=[
            jax.ShapeDtypeStruct((n, FOX_NQ), q_dtype),
            jax.ShapeDtypeStruct((n, FOX_NKV), F32),
            jax.ShapeDtypeStruct((n, FOX_NKV), F32),
            jax.ShapeDtypeStruct((grid[0], FOX_NKV, g1 * tm), F32),
            jax.ShapeDtypeStruct((grid[0], FOX_NKV, g1 * tm), F32),
            jax.ShapeDtypeStruct((FOX_KV_HEADS, n, LANES), BF16),
            jax.ShapeDtypeStruct((FOX_KV_HEADS, n, LANES), BF16),
            jax.ShapeDtypeStruct((FOX_KV_HEADS, FOX_HEAD_DIM + ONES_ROWS, n), BF16),
            jax.ShapeDtypeStruct((n, FOX_HEADS), F32),
            jax.ShapeDtypeStruct((FOX_HEADS, n), F32),
        ],
        scratch_shapes=[pltpu.VMEM((1, FOX_HEADS), F32), pltpu.VMEM((FOX_HEADS, 1), F32)],
        compiler_params=_params(("arbitrary", "arbitrary")),
        name="fox_proj",
    )(x, g.reshape(1, d), w_main, wft, b_f.reshape(1, FOX_HEADS), b_f.reshape(FOX_HEADS, 1), umat, lmat,
      place_a, place_b)
    return outs


def _online_softmax_step(s, v, m, l, acc):
    rows, n = acc.shape[0], s.shape[-1]
    m_new = jnp.maximum(m, jnp.max(s, axis=-1, keepdims=True))
    alpha = jnp.exp(m - m_new)
    p = jnp.exp(s - m_new)
    l_new = alpha * l + jnp.sum(p, axis=-1, keepdims=True)
    pv = _dot(p.reshape(rows, n).astype(BF16), v)
    return m_new, l_new, alpha.reshape(rows, 1) * acc + pv


def _local_softmax(s):
    m = jnp.max(s, axis=-1, keepdims=True)
    p = jnp.exp(s - m)
    return m, jnp.sum(p, axis=-1, keepdims=True), p


def _merge_softmax(m, l, acc, parts):
    rows = acc.shape[0]
    m_new = m
    for mu, _, _ in parts:
        m_new = jnp.maximum(m_new, mu)
    w = jnp.exp(m - m_new)
    l_new, acc_new = w * l, w.reshape(rows, 1) * acc
    for mu, lu, ou in parts:
        w = jnp.exp(mu - m_new)
        l_new, acc_new = l_new + w * lu, acc_new + w.reshape(rows, 1) * ou
    return m_new, l_new, acc_new


def _flash_tiles_t(tiles, logit_fn, mask_fn, vt1_fn, cqs, states):
    ngroups = len(states)
    sts = [[logit_fn(j, c) for c in range(ngroups)] for j in tiles]
    for j, row in zip(tiles, sts):
        if mask_fn is not None:
            row = [mask_fn(st, j) for st in row]
        probs = []
        for st, (m, _), cq in zip(row, states, cqs):
            mt = jnp.max(st, axis=0, keepdims=True)
            if cq is not None:
                mt = mt + cq
            m_new = jnp.maximum(m, mt)
            p = jnp.exp2(st - (m_new if cq is None else m_new - cq))
            probs.append((m_new, jnp.exp2(m - m_new), p.astype(BF16)))
        vt1 = vt1_fn(j)
        states = tuple((m_new, alpha * acc + _dot(vt1, p)) for (m_new, alpha, p), (_, acc) in zip(probs, states))
    return states


def _causal_flash_t(nfull, logit_fn, mask_fn, vt1_fn, cqs, states):
    run = lambda tiles, s, mask=None: _flash_tiles_t(tiles, logit_fn, mask, vt1_fn, cqs, s)
    states = lax.fori_loop(0, nfull // 2, lambda i, s: run([2 * i, 2 * i + 1], s), states)
    states = lax.cond(nfull % 2 == 1, lambda s: run([nfull - 1], s), lambda s: s, states)
    return run([nfull], states, mask_fn)


def _normalised_t(acc, dv):
    return acc[:dv] / acc[dv:dv + 1]


def _causal_mask_t(st, key0, row0, tq):
    key = key0 + lax.broadcasted_iota(jnp.int32, st.shape, 0)
    row = row0 + lax.rem(lax.broadcasted_iota(jnp.int32, st.shape, 1), tq)
    return jnp.where(key <= row, st, MASKED)


def _fox_attn_kernel(q_ref, ka_ref, kb_ref, vt_ref, ct_ref, o_ref, *, tq, tk):
    qi = pl.program_id(2)
    hd = FOX_HEAD_DIM
    q = q_ref[0]
    cq = ct_ref[0]
    lane = lax.broadcasted_iota(jnp.int32, (tq, LANES), 1)

    def q_operand(gi):
        grp = q[:, (gi // 2) * LANES:(gi // 2 + 1) * LANES]
        base = FOX_BIAS_PIECES * gi + (FOX_BIAS_LANE0 if gi % 2 == 0 else 0)
        minus = jnp.where((lane >= base) & (lane < base + FOX_BIAS_PIECES), -1.0, 0.0).astype(q.dtype)
        return jnp.where((lane < hd) if gi % 2 == 0 else (lane >= hd), grp, minus)

    heads = [((ka_ref, kb_ref)[gi % 2], q_operand(gi), cq[gi:gi + 1] * LOG2E) for gi in range(FOX_GROUP)]

    tile = lambda j: pl.ds(pl.multiple_of(j * tk, tk), tk)
    logit_fn = lambda j, c: _dot_nt(heads[c][0][0, 0, tile(j), :], heads[c][1])
    mask_fn = lambda st, j: _causal_mask_t(st, j * tk, qi * tq, tq)
    vt1_fn = lambda j: vt_ref[0, :, tile(j)]
    init = tuple((jnp.full((1, tq), MASKED, F32), jnp.zeros((hd + ONES_ROWS, tq), F32)) for _ in heads)
    states = _causal_flash_t((qi * tq) // tk, logit_fn, mask_fn, vt1_fn, [cqx for _, _, cqx in heads], init)
    ot = jnp.concatenate([_normalised_t(acc, hd) for _, acc in states], axis=0)
    o_ref[0] = ot.T.astype(o_ref.dtype)


def _fox_prompt_attention(q, ka, kb, vt, ct, *, batch, seq_len):
    n = batch * seq_len
    tq = _tile(seq_len, 512)
    tk = _tile(seq_len, 512)
    assert tk % tq == 0 and FOX_GROUP == 4
    nq = seq_len // tq
    gw = FOX_GROUP * FOX_HEAD_DIM
    q3 = q.reshape(batch, seq_len, FOX_NQ)
    ka4 = ka.reshape(FOX_KV_HEADS, batch, seq_len, LANES)
    kb4 = kb.reshape(FOX_KV_HEADS, batch, seq_len, LANES)
    ct3 = ct.reshape(FOX_KV_HEADS, FOX_GROUP, n)
    kspec = pl.BlockSpec((1, 1, seq_len, LANES), lambda b, h, i: (h, b, 0, 0))
    o = pl.pallas_call(
        functools.partial(_fox_attn_kernel, tq=tq, tk=tk),
        grid=(batch, FOX_KV_HEADS, nq),
        in_specs=[
            pl.BlockSpec((1, tq, gw), lambda b, h, i: (b, i, h)),
            kspec, kspec,
            pl.BlockSpec((1, FOX_HEAD_DIM + ONES_ROWS, seq_len), lambda b, h, i: (h, 0, b)),
            pl.BlockSpec((1, FOX_GROUP, tq), lambda b, h, i: (h, 0, b * nq + i)),
        ],
        out_specs=pl.BlockSpec((1, tq, gw), lambda b, h, i: (b, i, h)),
        out_shape=jax.ShapeDtypeStruct((batch, seq_len, FOX_NQ), BF16),
        compiler_params=_params(("parallel", "parallel", "arbitrary")),
        name="fox_prompt_attn",
    )(q3, ka4, kb4, vt, ct3)
    return o.reshape(n, FOX_NQ)


def _page_copies(pt_ref, b, chunk, slot, ppc, pairs):
    out = []
    for i in range(ppc):
        page = pt_ref[b, chunk * ppc + i]
        for hbm, buf, sem in pairs:
            out.append(pltpu.make_async_copy(hbm.at[page], buf.at[slot, i], sem.at[slot]))
    return out


def _paged_chunk_step(pt_ref, ci, nchunks, ppc, pairs, reverse):
    b, nb = pl.program_id(0), pl.num_programs(0)
    order = (lambda c: nchunks - 1 - c) if reverse else (lambda c: c)
    g = b * nchunks + ci
    ahead = PAGE_BUFFERS - 1

    def start(gs):
        for cp in _page_copies(pt_ref, gs // nchunks, order(gs % nchunks), gs % PAGE_BUFFERS, ppc, pairs):
            cp.start()

    for k in range(ahead):
        @pl.when(jnp.logical_and(g == 0, k < nb * nchunks))
        def _():
            start(g + k)

    @pl.when(g + ahead < nb * nchunks)
    def _():
        start(g + ahead)

    slot = g % PAGE_BUFFERS
    for cp in _page_copies(pt_ref, b, order(ci), slot, ppc, pairs):
        cp.wait()
    return order(ci), slot


def _pad_rows(x, rows):
    return jnp.concatenate([x, jnp.zeros((rows - x.shape[0], x.shape[1]), x.dtype)], axis=0)


def _suffix_sum_lanes(x):
    lane = lax.broadcasted_iota(jnp.int32, x.shape, 1)
    y = x
    sh = 1
    while sh < LANES:
        y = y + jnp.where(lane < LANES - sh, pltpu.roll(y, LANES - sh, axis=1), 0.0)
        sh *= 2
    return y


def _fox_sample_kernel(pt_ref, q_ref, kn_ref, vn_ref, dcol_ref, dtp_ref, kt_hbm, vt_hbm, lf_hbm,
                       o_ref, kbuf, vbuf, lbuf, ksem, vsem, lsem, *, ppc, nchunks, t_new):
    nk = ppc * PAGE_SIZE
    hd, g, heads = FOX_HEAD_DIM, FOX_GROUP, FOX_HEADS
    rows = heads * t_new
    pairs = ((kt_hbm, kbuf, ksem), (vt_hbm, vbuf, vsem), (lf_hbm, lbuf, lsem))

    q = q_ref[...]
    lane = lax.broadcasted_iota(jnp.int32, (t_new, FOX_NKV), 1)
    qrows = []
    for h in range(heads):
        kh, gi = divmod(h, g)
        grp = q[:, kh * FOX_NKV:(kh + 1) * FOX_NKV]
        shift = ((kh - gi) * hd) % FOX_NKV
        r = pltpu.roll(grp, shift, axis=1) if shift else grp
        qrows.append(jnp.where((lane >= kh * hd) & (lane < (kh + 1) * hd), r, 0.0))
    qbd = jnp.concatenate(qrows, axis=0).astype(BF16)
    dcol3 = dcol_ref[0].reshape(heads, t_new, 1)

    def body(ci, carry):
        m, l, acc, run = carry
        _, slot = _paged_chunk_step(pt_ref, ci, nchunks, ppc, pairs, reverse=True)
        s_pages = [None] * ppc
        for i in reversed(range(ppc)):
            lf = lbuf[slot, i]
            y = _suffix_sum_lanes(lf)
            bias = y - lf + run
            run = run + y[:, 0:1]
            s = _dot(qbd, kbuf[slot, i].astype(BF16))
            s_pages[i] = s.reshape(heads, t_new, PAGE_SIZE) + dcol3 + bias[:, None, :]
        pps = ppc // PAGE_GROUPS
        local = []
        for u in range(PAGE_GROUPS):
            mu, lu, pu = _local_softmax(jnp.concatenate(s_pages[u * pps:(u + 1) * pps], axis=2))
            local.append((mu, lu, pu.reshape(rows, pps * PAGE_SIZE).astype(BF16)))
        parts = []
        for u, (mu, lu, pu) in enumerate(local):
            ou = jnp.zeros((rows, FOX_NKV), F32)
            for i in range(pps):
                ou = ou + _dot_nt(pu[:, i * PAGE_SIZE:(i + 1) * PAGE_SIZE], vbuf[slot, u * pps + i].astype(BF16))
            parts.append((mu, lu, ou))
        return _merge_softmax(m, l, acc, parts) + (run,)

    init = (jnp.full((heads, t_new, 1), MASKED, F32), jnp.zeros((heads, t_new, 1), F32),
            jnp.zeros((rows, FOX_NKV), F32), jnp.zeros((heads, 1), F32))
    m, l, acc, _ = lax.fori_loop(0, nchunks, body, init)

    kn = _pad_rows(kn_ref[...], LANES).astype(BF16)
    vn = _pad_rows(vn_ref[...], LANES).astype(BF16)
    s3 = _dot_nt(qbd, kn).reshape(heads, t_new, LANES) + dcol3 - dtp_ref[0][:, None, :]
    trow = lax.broadcasted_iota(jnp.int32, (t_new, LANES), 0)
    tcol = lax.broadcasted_iota(jnp.int32, (t_new, LANES), 1)
    s3 = jnp.where((tcol <= trow)[None], s3, MASKED)
    m, l, acc = _online_softmax_step(s3, vn, m, l, acc)
    o = acc / l.reshape(rows, 1)

    groups = []
    for kh in range(FOX_KV_HEADS):
        tot = jnp.zeros((t_new, FOX_NKV), F32)
        for gi in range(g):
            h = kh * g + gi
            blk = o[h * t_new:(h + 1) * t_new]
            shift = ((gi - kh) * hd) % FOX_NKV
            r = pltpu.roll(blk, shift, axis=1) if shift else blk
            tot = tot + jnp.where((lane >= gi * hd) & (lane < (gi + 1) * hd), r, 0.0)
        groups.append(tot)
    o_ref[...] = jnp.concatenate(groups, axis=1)


def _fox_sample_attention(q, k_new, v_new, ct, page_table, kt_pages, vt_pages, lf_pages, *, t_new):
    ns = q.shape[0]
    bd, n_pages = page_table.shape
    assert bd * t_new == ns and t_new == SUBLANES
    ppc = _tile(n_pages, 32)
    nchunks = n_pages // ppc
    rows = FOX_HEADS * t_new
    d_bht = ct.reshape(FOX_HEADS, bd, t_new).transpose(1, 0, 2)
    dcol = d_bht.reshape(bd, rows, 1)
    dtp = jnp.pad(d_bht, ((0, 0), (0, 0), (0, LANES - t_new)))
    anyspec = pl.BlockSpec(memory_space=pl.ANY)
    return pl.pallas_call(
        functools.partial(_fox_sample_kernel, ppc=ppc, nchunks=nchunks, t_new=t_new),
        grid_spec=pltpu.PrefetchScalarGridSpec(
            num_scalar_prefetch=1,
            grid=(bd,),
            in_specs=[
                pl.BlockSpec((t_new, FOX_NQ), lambda b, pt: (b, 0)),
                pl.BlockSpec((t_new, FOX_NKV), lambda b, pt: (b, 0)),
                pl.BlockSpec((t_new, FOX_NKV), lambda b, pt: (b, 0)),
                pl.BlockSpec((1, rows, 1), lambda b, pt: (b, 0, 0)),
                pl.BlockSpec((1, FOX_HEADS, LANES), lambda b, pt: (b, 0, 0)),
                anyspec, anyspec, anyspec,
            ],
            out_specs=pl.BlockSpec((t_new, FOX_NQ), lambda b, pt: (b, 0)),
            scratch_shapes=[pltpu.VMEM((PAGE_BUFFERS, ppc, FOX_NKV, PAGE_SIZE), F32),
                            pltpu.VMEM((PAGE_BUFFERS, ppc, FOX_NKV, PAGE_SIZE), F32),
                            pltpu.VMEM((PAGE_BUFFERS, ppc, FOX_HEADS, PAGE_SIZE), F32),
                            pltpu.SemaphoreType.DMA((PAGE_BUFFERS,)),
                            pltpu.SemaphoreType.DMA((PAGE_BUFFERS,)),
                            pltpu.SemaphoreType.DMA((PAGE_BUFFERS,))],
        ),
        out_shape=jax.ShapeDtypeStruct((ns, FOX_NQ), F32),
        compiler_params=_params(("arbitrary",)),
        name="fox_sample_attn",
    )(page_table, q, k_new, v_new, dcol, dtp, kt_pages, vt_pages, lf_pages)


def _oproj_kernel(o_ref, wo_ref, x_ref, g_ref, out_ref):
    y = _dot(o_ref[...].astype(BF16), wo_ref[...])
    out_ref[...] = x_ref[...] + _rms(y, g_ref[...])


def _out_project(o, w_o, x, g):
    n, d = x.shape
    e = o.shape[1]
    tm = _tile(n, 512)
    row = lambda i: (i, 0)
    const = lambda i: (0, 0)
    return pl.pallas_call(
        _oproj_kernel,
        grid=(n // tm,),
        in_specs=[pl.BlockSpec((tm, e), row), pl.BlockSpec((e, d), const),
                  pl.BlockSpec((tm, d), row), pl.BlockSpec((1, d), const)],
        out_specs=pl.BlockSpec((tm, d), row),
        out_shape=jax.ShapeDtypeStruct((n, d), F32),
        compiler_params=_params(("parallel",)),
        name="out_proj",
    )(o, w_o.astype(BF16), x, g.reshape(1, d))


def _mla_oproj_kernel(ol_ref, wuv_ref, wo_ref, x_ref, g_ref, out_ref):
    parts = []
    for j in range(MLA_HEADS // 2):
        pair = jnp.concatenate([ol_ref[2 * j], ol_ref[2 * j + 1]], axis=1).astype(BF16)
        parts.append(_dot(pair, wuv_ref[j]).astype(BF16))
    y = _dot(jnp.concatenate(parts, axis=1), wo_ref[...])
    out_ref[...] = x_ref[...] + _rms(y, g_ref[...])


def _mla_out_project(o_lat, w_uv, w_o, x, g):
    n, d = x.shape
    tm = _tile(n, 512)
    npair = MLA_HEADS // 2
    wp = w_uv.reshape(MLA_KV_LORA, npair, 2, MLA_V_DIM).transpose(1, 2, 0, 3)
    z = jnp.zeros_like(wp[:, 0])
    wbd = jnp.concatenate([jnp.concatenate([wp[:, 0], z], axis=2),
                           jnp.concatenate([z, wp[:, 1]], axis=2)], axis=1).astype(BF16)
    row = lambda i: (i, 0)
    const = lambda i: (0, 0)
    return pl.pallas_call(
        _mla_oproj_kernel,
        grid=(n // tm,),
        in_specs=[pl.BlockSpec((MLA_HEADS, tm, MLA_KV_LORA), lambda i: (0, i, 0)),
                  pl.BlockSpec(wbd.shape, lambda i: (0, 0, 0)),
                  pl.BlockSpec(w_o.shape, const),
                  pl.BlockSpec((tm, d), row), pl.BlockSpec((1, d), const)],
        out_specs=pl.BlockSpec((tm, d), row),
        out_shape=jax.ShapeDtypeStruct((n, d), F32),
        compiler_params=_params(("parallel",)),
        name="mla_out_proj",
    )(o_lat, wbd, w_o.astype(BF16), x, g.reshape(1, d))


def _mla_proj_kernel(x_ref, g_ref, wa_ref, qn_ref, kvn_ref, wnope_ref, wpe_ref, wper_ref, wuk_ref,
                     cos_ref, sin_ref, qp_ref, kvp_ref, ckv_ref, ckvt_ref, kpe_ref, *, logit_scale):
    tm = x_ref.shape[0]
    scale = (MLA_NOPE + MLA_ROPE) ** -0.5 * logit_scale
    h = _rms(x_ref[...], g_ref[...]).astype(BF16)
    a = _dot(h, wa_ref[...])
    cos, sin = cos_ref[...], sin_ref[...]
    c_q = _rms(a[:, :MLA_Q_LORA], qn_ref[...]).astype(BF16)
    c_kv = _rms(a[:, MLA_Q_LORA:MLA_Q_LORA + MLA_KV_LORA], kvn_ref[...])
    r0 = MLA_Q_LORA + MLA_KV_LORA
    kpe_rep = a[:, r0:r0 + LANES] * cos + a[:, r0 + LANES:r0 + 2 * LANES] * sin
    ckv_ref[...] = c_kv
    ckvt_ref[0:MLA_KV_LORA, :] = c_kv.T.astype(BF16)
    ckvt_ref[MLA_KV_LORA:MLA_KV_LORA + ONES_ROWS, :] = jnp.ones((ONES_ROWS, tm), BF16)
    kpe_ref[...] = kpe_rep[:, :MLA_ROPE]
    kvp_ref[...] = jnp.concatenate([c_kv, kpe_rep], axis=1).astype(BF16)

    nrep = MLA_HEADS // ROPE_REP
    cos_all = jnp.concatenate([cos] * nrep, axis=1)
    sin_all = jnp.concatenate([sin] * nrep, axis=1)
    qpe = (_dot(c_q, wpe_ref[...]) * cos_all + _dot(c_q, wper_ref[...]) * sin_all) * scale
    qnope = _dot(c_q, wnope_ref[...]).astype(BF16)
    lane = lax.broadcasted_iota(jnp.int32, (tm, LANES), 1)
    for j in range(MLA_HEADS // 2):
        qlat2 = _dot(qnope[:, j * LANES:(j + 1) * LANES], wuk_ref[j]) * scale
        for w in range(2):
            hh = 2 * j + w
            grp = qpe[:, (hh // ROPE_REP) * LANES:(hh // ROPE_REP + 1) * LANES]
            quarter = hh % ROPE_REP
            pe = jnp.where((lane >= quarter * MLA_ROPE) & (lane < (quarter + 1) * MLA_ROPE), grp, 0.0)
            qp_ref[hh] = jnp.concatenate([qlat2[:, w * LANES:(w + 1) * LANES], pe], axis=1).astype(qp_ref.dtype)


def _rot_half_cols(w):
    half = MLA_ROPE // 2
    return jnp.concatenate([-w[..., half:], w[..., :half]], axis=-1)


def _mla_project(x, g, w_a, q_norm, kv_norm, w_uq, w_uk, pos, *, q_dtype, logit_scale):
    n, d = x.shape
    tm = _tile(n, 256)
    npos = pos.shape[0]
    assert n % npos == 0 and npos % tm == 0
    grid = (n // npos, npos // tm)
    g1 = grid[1]

    half = MLA_ROPE // 2
    inv = ROPE_THETA ** (-jnp.arange(half, dtype=F32) / half)
    ang = pos.astype(F32)[:, None] * inv[None, :]
    cos = jnp.tile(jnp.cos(ang), (1, 2 * ROPE_REP))
    sin = jnp.tile(jnp.sin(ang), (1, 2 * ROPE_REP))

    r0 = MLA_Q_LORA + MLA_KV_LORA
    w_pe = w_a[:, r0:]
    wa_ext = jnp.concatenate([w_a[:, :r0], jnp.tile(w_pe, (1, ROPE_REP)),
                              jnp.tile(_rot_half_cols(w_pe), (1, ROPE_REP))], axis=1).astype(BF16)
    w_nope = w_uq[:, :, :MLA_NOPE].reshape(MLA_Q_LORA, MLA_HEADS * MLA_NOPE).astype(BF16)
    w_qpe = w_uq[:, :, MLA_NOPE:]
    w_pe_q = w_qpe.reshape(MLA_Q_LORA, MLA_HEADS * MLA_ROPE).astype(BF16)
    w_pe_qr = _rot_half_cols(w_qpe).reshape(MLA_Q_LORA, MLA_HEADS * MLA_ROPE).astype(BF16)
    npair = MLA_HEADS // 2
    wk = w_uk.reshape(MLA_KV_LORA, npair, 2, MLA_NOPE).transpose(1, 2, 3, 0)
    z = jnp.zeros_like(wk[:, 0])
    wuk_bd = jnp.concatenate([jnp.concatenate([wk[:, 0], z], axis=2),
                              jnp.concatenate([z, wk[:, 1]], axis=2)], axis=1).astype(BF16)

    tok = lambda i, j: (i * g1 + j, 0)
    const = lambda i, j: (0, 0)
    posb = lambda i, j: (j, 0)
    return pl.pallas_call(
        functools.partial(_mla_proj_kernel, logit_scale=logit_scale),
        grid=grid,
        in_specs=[
            pl.BlockSpec((tm, d), tok),
            pl.BlockSpec((1, d), const),
            pl.BlockSpec(wa_ext.shape, const),
            pl.BlockSpec((1, MLA_Q_LORA), const),
            pl.BlockSpec((1, MLA_KV_LORA), const),
            pl.BlockSpec(w_nope.shape, const),
            pl.BlockSpec(w_pe_q.shape, const),
            pl.BlockSpec(w_pe_qr.shape, const),
            pl.BlockSpec(wuk_bd.shape, lambda i, j: (0, 0, 0)),
            pl.BlockSpec((tm, LANES), posb),
            pl.BlockSpec((tm, LANES), posb),
        ],
        out_specs=[
            pl.BlockSpec((MLA_HEADS, tm, MLA_QK), lambda i, j: (0, i * g1 + j, 0)),
            pl.BlockSpec((tm, MLA_QK), tok),
            pl.BlockSpec((tm, MLA_KV_LORA), tok),
            pl.BlockSpec((MLA_KV_LORA + ONES_ROWS, tm), lambda i, j: (0, i * g1 + j)),
            pl.BlockSpec((tm, MLA_ROPE), tok),
        ],
        out_shape=[
            jax.ShapeDtypeStruct((MLA_HEADS, n, MLA_QK), q_dtype),
            jax.ShapeDtypeStruct((n, MLA_QK), BF16),
            jax.ShapeDtypeStruct((n, MLA_KV_LORA), F32),
            jax.ShapeDtypeStruct((MLA_KV_LORA + ONES_ROWS, n), BF16),
            jax.ShapeDtypeStruct((n, MLA_ROPE), F32),
        ],
        compiler_params=_params(("parallel", "parallel")),
        name="mla_proj",
    )(x, g.reshape(1, d), wa_ext, q_norm.reshape(1, -1), kv_norm.reshape(1, -1),
      w_nope, w_pe_q, w_pe_qr, wuk_bd, cos, sin)


def _mla_attn_kernel(q_ref, kv_ref, ckvt_ref, o_ref, *, tq, tk):
    qi = pl.program_id(1)
    nh = MLA_HEADS
    ngroups = 4
    cols = nh * tq // ngroups
    q = q_ref[...].reshape(nh * tq, MLA_QK)
    qs = [q[i * cols:(i + 1) * cols] for i in range(ngroups)]
    tile = lambda j: pl.ds(pl.multiple_of(j * tk, tk), tk)
    logit_fn = lambda j, c: _dot_nt(kv_ref[0, tile(j), :], qs[c])
    mask_fn = lambda st, j: _causal_mask_t(st, j * tk, qi * tq, tq)
    vt1_fn = lambda j: ckvt_ref[:, tile(j)]
    init = tuple((jnp.full((1, cols), MASKED, F32), jnp.zeros((MLA_KV_LORA + ONES_ROWS, cols), F32))
                 for _ in range(ngroups))
    states = _causal_flash_t((qi * tq) // tk, logit_fn, mask_fn, vt1_fn, [None] * ngroups, init)
    ot = jnp.concatenate([_normalised_t(acc, MLA_KV_LORA) for _, acc in states], axis=1)
    o_ref[...] = ot.T.reshape(nh, tq, MLA_KV_LORA).astype(o_ref.dtype)


def _mla_prompt_attention(qp, kvp, ckvt, *, batch, seq_len):
    n = batch * seq_len
    tq = _tile(seq_len, 128)
    tk = _tile(seq_len, 512)
    assert tk % tq == 0
    nq = seq_len // tq
    kv3 = kvp.reshape(batch, seq_len, MLA_QK)
    return pl.pallas_call(
        functools.partial(_mla_attn_kernel, tq=tq, tk=tk),
        grid=(batch, nq),
        in_specs=[pl.BlockSpec((MLA_HEADS, tq, MLA_QK), lambda b, i: (0, b * nq + i, 0)),
                  pl.BlockSpec((1, seq_len, MLA_QK), lambda b, i: (b, 0, 0)),
                  pl.BlockSpec((MLA_KV_LORA + ONES_ROWS, seq_len), lambda b, i: (0, b))],
        out_specs=pl.BlockSpec((MLA_HEADS, tq, MLA_KV_LORA), lambda b, i: (0, b * nq + i, 0)),
        out_shape=jax.ShapeDtypeStruct((MLA_HEADS, n, MLA_KV_LORA), BF16),
        compiler_params=_params(("parallel", "arbitrary")),
        name="mla_prompt_attn",
    )(qp, kv3, ckvt)


def _mla_sample_kernel(pt_ref, q_ref, cn_ref, rn_ref, c_hbm, r_hbm, o_ref, cbuf, rbuf, csem, rsem,
                       *, ppc, nchunks, t_new):
    nk = ppc * PAGE_SIZE
    nh = MLA_HEADS
    rows = nh * t_new
    pairs = ((c_hbm, cbuf, csem), (r_hbm, rbuf, rsem))

    q = q_ref[...].reshape(rows, MLA_QK)
    qlat = q[:, :MLA_KV_LORA].astype(BF16)
    grp = q[:, MLA_KV_LORA:]
    qpe = grp[:, :MLA_ROPE]
    for i in range(1, ROPE_REP):
        qpe = qpe + grp[:, i * MLA_ROPE:(i + 1) * MLA_ROPE]
    qpe = qpe.astype(BF16)

    def body(c, carry):
        _, slot = _paged_chunk_step(pt_ref, c, nchunks, ppc, pairs, reverse=False)
        pps = ppc // PAGE_GROUPS
        nsub = pps * PAGE_SIZE
        logits = []
        for u in range(PAGE_GROUPS):
            ckv = cbuf[slot, u * pps:(u + 1) * pps].reshape(nsub, MLA_KV_LORA).astype(BF16)
            s_pe = jnp.concatenate([_dot(qpe, rbuf[slot, u * pps + i].astype(BF16)) for i in range(pps)], axis=1)
            logits.append((ckv, (_dot_nt(qlat, ckv) + s_pe).reshape(nh, t_new, nsub)))
        local = [(ckv,) + _local_softmax(s3) for ckv, s3 in logits]
        parts = [(mu, lu, _dot(pu.reshape(rows, nsub).astype(BF16), ckv)) for ckv, mu, lu, pu in local]
        return _merge_softmax(*carry, parts)

    init = (jnp.full((nh, t_new, 1), MASKED, F32), jnp.zeros((nh, t_new, 1), F32),
            jnp.zeros((rows, MLA_KV_LORA), F32))
    m, l, acc = lax.fori_loop(0, nchunks, body, init)

    cn = _pad_rows(cn_ref[...], LANES).astype(BF16)
    rn = _pad_rows(rn_ref[...], LANES).astype(BF16)
    s3 = (_dot_nt(qlat, cn) + _dot_nt(qpe, rn)).reshape(nh, t_new, LANES)
    trow = lax.broadcasted_iota(jnp.int32, (t_new, LANES), 0)
    tcol = lax.broadcasted_iota(jnp.int32, (t_new, LANES), 1)
    s3 = jnp.where((tcol <= trow)[None], s3, MASKED)
    m, l, acc = _online_softmax_step(s3, cn, m, l, acc)
    o_ref[...] = (acc / l.reshape(rows, 1)).reshape(nh, t_new, MLA_KV_LORA)


def _mla_sample_attention(qp, c_new, kpe_new, page_table, ckv_pages, kpet_pages, *, t_new):
    ns = c_new.shape[0]
    bd, n_pages = page_table.shape
    assert bd * t_new == ns and t_new == SUBLANES
    ppc = _tile(n_pages, 32)
    nchunks = n_pages // ppc
    return pl.pallas_call(
        functools.partial(_mla_sample_kernel, ppc=ppc, nchunks=nchunks, t_new=t_new),
        grid_spec=pltpu.PrefetchScalarGridSpec(
            num_scalar_prefetch=1,
            grid=(bd,),
            in_specs=[
                pl.BlockSpec((MLA_HEADS, t_new, MLA_QK), lambda b, pt: (0, b, 0)),
                pl.BlockSpec((t_new, MLA_KV_LORA), lambda b, pt: (b, 0)),
                pl.BlockSpec((t_new, MLA_ROPE), lambda b, pt: (b, 0)),
                pl.BlockSpec(memory_space=pl.ANY),
                pl.BlockSpec(memory_space=pl.ANY),
            ],
            out_specs=pl.BlockSpec((MLA_HEADS, t_new, MLA_KV_LORA), lambda b, pt: (0, b, 0)),
            scratch_shapes=[pltpu.VMEM((PAGE_BUFFERS, ppc, PAGE_SIZE, MLA_KV_LORA), F32),
                            pltpu.VMEM((PAGE_BUFFERS, ppc, MLA_ROPE, PAGE_SIZE), F32),
                            pltpu.SemaphoreType.DMA((PAGE_BUFFERS,)),
                            pltpu.SemaphoreType.DMA((PAGE_BUFFERS,))],
        ),
        out_shape=jax.ShapeDtypeStruct((MLA_HEADS, ns, MLA_KV_LORA), F32),
        compiler_params=_params(("arbitrary",)),
        name="mla_sample_attn",
    )(page_table, qp, c_new, kpe_new, ckv_pages, kpet_pages)


def _ffn_kernel(x_ref, g1_ref, g2_ref, win_ref, cw_ref, cb_ref, wout_ref, *rest, fc, seq_rows, chained):
    if chained:
        out_ref, st_out_ref, act_ref, carry_ref = rest
    else:
        st_ref, out_ref, st_out_ref, act_ref = rest
    tm, d = x_ref.shape
    dff = wout_ref.shape[0]
    x = x_ref[...]
    h = _rms(x, g1_ref[...]).astype(BF16)

    if chained:
        @pl.when(pl.program_id(1) == 0)
        def _():
            carry_ref[...] = jnp.zeros_like(carry_ref)
        row = lax.broadcasted_iota(jnp.int32, (tm, fc), 0)
    else:
        nseq = tm // seq_rows
        row = lax.broadcasted_iota(jnp.int32, (nseq, seq_rows, fc), 1)

    for f in range(dff // fc):
        cs = slice(f * fc, (f + 1) * fc)
        gate = _dot(h, win_ref[:, cs])
        up = _dot(h, win_ref[:, dff + f * fc:dff + (f + 1) * fc])
        w0, w1, w2 = cw_ref[0:1, cs], cw_ref[1:2, cs], cw_ref[2:3, cs]
        if chained:
            p0, p1 = carry_ref[0:1, cs], carry_ref[1:2, cs]
            sh1 = jnp.where(row == 0, p1, pltpu.roll(gate, 1, axis=0))
            sh2 = jnp.where(row == 0, p0, jnp.where(row == 1, p1, pltpu.roll(gate, 2, axis=0)))
            carry_ref[0:2, cs] = gate[tm - 2:tm]
            st_out_ref[0, :, cs] = gate[tm - 2:tm]
            conv = cb_ref[:, cs] + w0 * sh2 + w1 * sh1 + w2 * gate
        else:
            g3 = gate.reshape(nseq, seq_rows, fc)
            p0, p1 = st_ref[:, 0:1, cs], st_ref[:, 1:2, cs]
            sh1 = jnp.where(row == 0, p1, pltpu.roll(g3, 1, axis=1))
            sh2 = jnp.where(row == 0, p0, jnp.where(row == 1, p1, pltpu.roll(g3, 2, axis=1)))
            st_out_ref[:, :, cs] = g3[:, seq_rows - 2:seq_rows, :]
            conv = (cb_ref[:, cs] + w0 * sh2 + w1 * sh1 + w2 * g3).reshape(tm, fc)
        act_ref[:, cs] = (jax.nn.gelu(conv, approximate=True) * up).astype(BF16)

    y = _dot(act_ref[...], wout_ref[...])
    out_ref[...] = x + _rms(y, g2_ref[...])


def _conv_ffn_block(x, g1, g2, w_in, conv_w, conv_b, w_out, state, *, seq_len):
    n, d = x.shape
    dff = w_out.shape[0]
    fc = 256
    assert dff % fc == 0 and CONV_W == 3
    nseq_total = n // seq_len
    chained = state is None
    wspec = dict(pipeline_mode=pl.Buffered(1))
    if chained:
        tm = _tile(seq_len, 512)
        grid = (nseq_total, seq_len // tm)
        g1n = grid[1]
        tok = lambda i, j: (i * g1n + j, 0)
        const = lambda i, j: (0, 0)
        st_spec = pl.BlockSpec((1, CONV_W - 1, dff), lambda i, j: (i, 0, 0))
        extra_in, extra_specs = [], []
        scratch = [pltpu.VMEM((tm, dff), BF16), pltpu.VMEM((SUBLANES, dff), F32)]
        sem = ("arbitrary", "arbitrary")
    else:
        sb = _tile(nseq_total, 32)
        tm = sb * seq_len
        grid = (nseq_total // sb,)
        tok = lambda i: (i, 0)
        const = lambda i: (0, 0)
        st_spec = pl.BlockSpec((sb, CONV_W - 1, dff), lambda i: (i, 0, 0))
        extra_in, extra_specs = [state], [st_spec]
        scratch = [pltpu.VMEM((tm, dff), BF16)]
        sem = ("arbitrary",)
    out, st_new = pl.pallas_call(
        functools.partial(_ffn_kernel, fc=fc, seq_rows=seq_len, chained=chained),
        grid=grid,
        in_specs=[
            pl.BlockSpec((tm, d), tok),
            pl.BlockSpec((1, d), const),
            pl.BlockSpec((1, d), const),
            pl.BlockSpec((d, 2 * dff), const, **wspec),
            pl.BlockSpec((CONV_W, dff), const),
            pl.BlockSpec((1, dff), const),
            pl.BlockSpec((dff, d), const, **wspec),
        ] + extra_specs,
        out_specs=[pl.BlockSpec((tm, d), tok), st_spec],
        out_shape=[jax.ShapeDtypeStruct((n, d), F32),
                   jax.ShapeDtypeStruct((nseq_total, CONV_W - 1, dff), F32)],
        scratch_shapes=scratch,
        compiler_params=_params(sem),
        name="conv_ffn",
    )(x, g1.reshape(1, d), g2.reshape(1, d), w_in.astype(BF16), conv_w, conv_b.reshape(1, dff),
      w_out.astype(BF16), *extra_in)
    return out, st_new


def kernel(x_prompt, x_sample, cache_fox_k, cache_fox_v, cache_fox_logf, cache_mla_ckv, cache_mla_kpe, state_conv, page_table, norm_mix_pre, norm_mix_post, norm_ffn_pre, norm_ffn_post, fox_w_in, fox_b_f, fox_w_o, mla_w_a, mla_q_norm, mla_kv_norm, mla_w_uq, mla_w_uk, mla_w_uv, mla_w_o, ffn_w_in, ffn_conv_w, ffn_conv_b, ffn_w_out):
    batch, seq_len, d = x_prompt.shape
    bd, t_new, _ = x_sample.shape
    depth = norm_mix_pre.shape[0]
    n_pool = cache_fox_k.shape[1]
    past = page_table.shape[1] * PAGE_SIZE
    xp = x_prompt.reshape(batch * seq_len, d)
    xs = x_sample.reshape(bd * t_new, d)

    fox_kt_pages = jnp.transpose(cache_fox_k, (0, 1, 3, 4, 2)).reshape(-1, FOX_NKV, PAGE_SIZE)
    fox_vt_pages = jnp.transpose(cache_fox_v, (0, 1, 3, 4, 2)).reshape(-1, FOX_NKV, PAGE_SIZE)
    fox_f_pages = jnp.transpose(cache_fox_logf, (0, 1, 3, 2)).reshape(-1, FOX_HEADS, PAGE_SIZE)
    mla_c_pages = cache_mla_ckv.reshape(-1, PAGE_SIZE, MLA_KV_LORA)
    mla_rt_pages = jnp.transpose(cache_mla_kpe, (0, 1, 3, 2)).reshape(-1, MLA_ROPE, PAGE_SIZE)

    outs_p = {k: [] for k in ("fk", "fv", "ff", "mc", "mr", "cv")}
    outs_s = {k: [] for k in ("fk", "fv", "ff", "mc", "mr", "cv")}
    for i in range(depth):
        li = i // N_MIXERS
        pages = page_table + li * n_pool
        if i % N_MIXERS == 0:
            w_in, b_f, w_o = fox_w_in[li], fox_b_f[li], fox_w_o[li]
            q, _, _, ktf, vtf, ka, kb, vt, logf, ct = _fox_project(xp, norm_mix_pre[i], w_in, b_f, seq_len=seq_len,
                                                                   q_dtype=BF16, logit_scale=LOG2E)
            o = _fox_prompt_attention(q, ka, kb, vt, ct, batch=batch, seq_len=seq_len)
            xp = _out_project(o, w_o, xp, norm_mix_post[i])
            heads_t = lambda a: a.reshape(batch, FOX_KV_HEADS, FOX_HEAD_DIM, seq_len).transpose(0, 3, 1, 2)
            outs_p["fk"].append(heads_t(ktf))
            outs_p["fv"].append(heads_t(vtf))
            outs_p["ff"].append(logf.reshape(batch, seq_len, FOX_HEADS))

            q, k, v, _, _, _, _, _, logf, ct = _fox_project(xs, norm_mix_pre[i], w_in, b_f, seq_len=t_new,
                                                            q_dtype=F32, logit_scale=1.0)
            o = _fox_sample_attention(q, k, v, ct, pages, fox_kt_pages, fox_vt_pages, fox_f_pages, t_new=t_new)
            xs = _out_project(o, w_o, xs, norm_mix_post[i])
            outs_s["fk"].append(k.reshape(bd, t_new, FOX_KV_HEADS, FOX_HEAD_DIM))
            outs_s["fv"].append(v.reshape(bd, t_new, FOX_KV_HEADS, FOX_HEAD_DIM))
            outs_s["ff"].append(logf.reshape(bd, t_new, FOX_HEADS))
        else:
            args = (mla_w_a[li], mla_q_norm[li], mla_kv_norm[li], mla_w_uq[li], mla_w_uk[li])
            qp, kvp, ckv, ckvt, kpe = _mla_project(xp, norm_mix_pre[i], *args, jnp.arange(seq_len),
                                                   q_dtype=BF16, logit_scale=LOG2E)
            o_lat = _mla_prompt_attention(qp, kvp, ckvt, batch=batch, seq_len=seq_len)
            xp = _mla_out_project(o_lat, mla_w_uv[li], mla_w_o[li], xp, norm_mix_post[i])
            outs_p["mc"].append(ckv.reshape(batch, seq_len, MLA_KV_LORA))
            outs_p["mr"].append(kpe.reshape(batch, seq_len, MLA_ROPE))

            pos_s = jnp.tile(past + jnp.arange(t_new), bd)
            qp, _, ckv, _, kpe = _mla_project(xs, norm_mix_pre[i], *args, pos_s, q_dtype=F32, logit_scale=1.0)
            o_lat = _mla_sample_attention(qp, ckv, kpe, pages, mla_c_pages, mla_rt_pages, t_new=t_new)
            xs = _mla_out_project(o_lat, mla_w_uv[li], mla_w_o[li], xs, norm_mix_post[i])
            outs_s["mc"].append(ckv.reshape(bd, t_new, MLA_KV_LORA))
            outs_s["mr"].append(kpe.reshape(bd, t_new, MLA_ROPE))

        ffn = (norm_ffn_pre[i], norm_ffn_post[i], ffn_w_in[i], ffn_conv_w[i], ffn_conv_b[i], ffn_w_out[i])
        xp, cvp = _conv_ffn_block(xp, *ffn, None, seq_len=seq_len)
        xs, cvs = _conv_ffn_block(xs, *ffn, state_conv[i], seq_len=t_new)
        outs_p["cv"].append(cvp)
        outs_s["cv"].append(cvs)

    ldt = cache_fox_logf.dtype
    st = jnp.stack
    return (xp.reshape(batch, seq_len, d), xs.reshape(bd, t_new, d),
            st(outs_p["fk"]), st(outs_p["fv"]), st(outs_p["ff"]).astype(ldt), st(outs_p["mc"]), st(outs_p["mr"]), st(outs_p["cv"]),
            st(outs_s["fk"]), st(outs_s["fv"]), st(outs_s["ff"]).astype(ldt), st(outs_s["mc"]), st(outs_s["mr"]), st(outs_s["cv"]))
```
